```python
import math
import jax, jax.numpy as jnp
from jax import lax
import numpy as np

D_MODEL = 1024
BATCH = 4
SEQ = 4096
DEPTH = 4

CHUNK = 64
N_MIXERS = 3
N_LAYERS_A = (DEPTH + 2) // 3
N_LAYERS_B = (DEPTH + 1) // 3
N_LAYERS_C = DEPTH // 3

DEEPNORM_ALPHA = (2.0 * DEPTH) ** 0.25
DEEPNORM_BETA = (8.0 * DEPTH) ** -0.25
LN_EPS = 1e-5
RMS_EPS = 1e-6

MLA_HEADS = 8
MLA_NOPE = 128
MLA_ROPE = 64
MLA_V = 128
MLA_Q_LORA = 384
MLA_KV_LORA = 256
MLA_DOWN = MLA_Q_LORA + MLA_KV_LORA + MLA_ROPE
ROPE_THETA = 10000.0
Q_BLOCK = 128

POOL_WINDOWS = (2, 4, 8, 16)
POOL_GROUPS = len(POOL_WINDOWS)
POOL_GROUP_DIM = D_MODEL // POOL_GROUPS

GDN_QK_HEADS = 8
GDN_V_HEADS = 16
GDN_DK = 128
GDN_DV = 128
GDN_CONV = 4
GDN_Q_DIM = GDN_QK_HEADS * GDN_DK
GDN_V_DIM = GDN_V_HEADS * GDN_DV
GDN_CONV_DIM = 2 * GDN_Q_DIM + GDN_V_DIM
GDN_PROJ_DIM = GDN_CONV_DIM + GDN_V_DIM + 2 * GDN_V_HEADS

N_EXPERTS = 32
TOP_K = 4
D_EXPERT = D_MODEL
SWIGLU_LIMIT = 7.0
SWIGLU_ALPHA = 1.702
EXPERT_BLOCK = 128

kernel_name = "hybrid_mla_pool_gdn_moe_deepnorm_adaln"


def layer_norm(x, g, b):
    xf = x.astype(jnp.float32)
    mu = jnp.mean(xf, axis=-1, keepdims=True)
    var = jnp.mean(jnp.square(xf - mu), axis=-1, keepdims=True)
    return ((xf - mu) * lax.rsqrt(var + LN_EPS) * g + b).astype(x.dtype)


def rms_norm(x, g):
    xf = x.astype(jnp.float32)
    return (xf * lax.rsqrt(jnp.mean(jnp.square(xf), axis=-1, keepdims=True) + RMS_EPS) * g).astype(x.dtype)


def l2_normalize(x):
    return x * lax.rsqrt(jnp.sum(jnp.square(x), axis=-1, keepdims=True) + RMS_EPS)


def rope_tables(seq_len):
    pos = jnp.arange(seq_len, dtype=jnp.float32)
    inv_freq = ROPE_THETA ** (-jnp.arange(0, MLA_ROPE, 2, dtype=jnp.float32) / MLA_ROPE)
    ang = pos[:, None] * inv_freq[None, :]
    return jnp.cos(ang), jnp.sin(ang)


def apply_rotary(x, cos, sin):
    half = x.shape[-1] // 2
    x1, x2 = x[..., :half].astype(jnp.float32), x[..., half:].astype(jnp.float32)
    return jnp.concatenate([x1 * cos - x2 * sin, x2 * cos + x1 * sin], axis=-1).astype(x.dtype)


def mla_mixer(h, cos, sin, w_dqkv, g_q, g_kv, w_uq, w_ukv, w_o):
    B, S, _ = h.shape
    down = h @ w_dqkv
    cq, ckv, k_rope = jnp.split(down, [MLA_Q_LORA, MLA_Q_LORA + MLA_KV_LORA], axis=-1)
    q = (rms_norm(cq, g_q) @ w_uq).reshape(B, S, MLA_HEADS, MLA_NOPE + MLA_ROPE)
    q_nope = q[..., :MLA_NOPE]
    q_rope = apply_rotary(q[..., MLA_NOPE:], cos[None, :, None, :], sin[None, :, None, :])
    k_rope = apply_rotary(k_rope, cos[None], sin[None])
    kv = (rms_norm(ckv, g_kv) @ w_ukv).reshape(B, S, MLA_HEADS, MLA_NOPE + MLA_V)
    k_nope, v = kv[..., :MLA_NOPE], kv[..., MLA_NOPE:]
    scale = (MLA_NOPE + MLA_ROPE) ** -0.5
    n_blk = S // Q_BLOCK
    qn_b = q_nope.reshape(B, n_blk, Q_BLOCK, MLA_HEADS, MLA_NOPE).swapaxes(0, 1)
    qr_b = q_rope.reshape(B, n_blk, Q_BLOCK, MLA_HEADS, MLA_ROPE).swapaxes(0, 1)
    key_chunk = jnp.arange(S) // CHUNK

    def attend(args):
        qn, qr, blk = args
        s = (jnp.einsum('bqhd,bkhd->bhqk', qn, k_nope, preferred_element_type=jnp.float32)
             + jnp.einsum('bqhr,bkr->bhqk', qr, k_rope, preferred_element_type=jnp.float32)) * scale
        q_chunk = (blk * Q_BLOCK + jnp.arange(Q_BLOCK)) // CHUNK
        mask = key_chunk[None, :] <= q_chunk[:, None]
        s = jnp.where(mask, s, -jnp.inf)
        p = jax.nn.softmax(s, axis=-1).astype(v.dtype)
        return jnp.einsum('bhqk,bkhd->bqhd', p, v)

    o = lax.map(attend, (qn_b, qr_b, jnp.arange(n_blk)))
    o = o.swapaxes(0, 1).reshape(B, S, MLA_HEADS * MLA_V)
    return o @ w_o


def pool_mixer(h, w_pool, ch_scale):
    B, S, D = h.shape
    hf = h.astype(jnp.float32)
    cs = jnp.concatenate([jnp.zeros((B, 1, D), jnp.float32), jnp.cumsum(hf, axis=1)], axis=1)
    hi = jnp.arange(1, S + 1)
    upper = cs[:, 1:]
    groups = []
    for g, win in enumerate(POOL_WINDOWS):
        sl = slice(g * POOL_GROUP_DIM, (g + 1) * POOL_GROUP_DIM)
        lo = jnp.maximum(hi - win, 0)
        count = (hi - lo).astype(jnp.float32)[None, :, None]
        mean = (upper[:, :, sl] - cs[:, lo, sl]) / count
        groups.append(mean - hf[:, :, sl])
    d = jnp.stack(groups, axis=2).astype(h.dtype)
    y = jnp.einsum('bsgc,gcd->bsgd', d, w_pool).reshape(B, S, D)
    return y * ch_scale


def causal_depthwise_conv(x, w):
    k_w, ch = w.shape
    return lax.conv_general_dilated(x, w[:, None, :], window_strides=(1,), padding=((k_w - 1, 0),),
                                    dimension_numbers=('NWC', 'WIO', 'NWC'), feature_group_count=ch)


def gated_delta_rule_chunked(q, k, v, beta, g):
    B, S, H, DK = q.shape
    DV = v.shape[-1]
    N = S // CHUNK

    def to_chunks(t):
        t = t.reshape(B, N, CHUNK, H, *t.shape[3:])
        return jnp.moveaxis(t, 3, 1)

    q, k, v, beta, g = (to_chunks(t) for t in (q, k, v, beta, g))
    G = jnp.cumsum(g, axis=-1)
    tril_incl = jnp.tril(jnp.ones((CHUNK, CHUNK), bool))
    tril_strict = jnp.tril(jnp.ones((CHUNK, CHUNK), bool), -1)
    decay = jnp.exp(jnp.where(tril_incl, G[..., :, None] - G[..., None, :], -jnp.inf))
    k_beta = k * beta[..., None]
    v_beta = v * beta[..., None]
    m = jnp.where(tril_strict, jnp.einsum('bhnid,bhnjd->bhnij', k_beta, k) * decay, 0.0)
    eye = jnp.eye(CHUNK, dtype=jnp.float32)
    t_inv = lax.linalg.triangular_solve(eye + m, jnp.broadcast_to(eye, m.shape), left_side=True,
                                        lower=True, unit_diagonal=True)
    u = t_inv @ v_beta
    w = t_inv @ (k_beta * jnp.exp(G)[..., None])
    qg = q * jnp.exp(G)[..., None]
    a_intra = jnp.einsum('bhnid,bhnjd->bhnij', q, k) * decay
    kd = k * jnp.exp(G[..., -1:] - G)[..., None]
    g_last = jnp.exp(G[..., -1])

    def step(state, xs):
        qg_n, w_n, u_n, a_n, kd_n, gl_n = xs
        v_new = u_n - jnp.einsum('bhcd,bhde->bhce', w_n, state)
        o_n = jnp.einsum('bhcd,bhde->bhce', qg_n, state) + jnp.einsum('bhij,bhje->bhie', a_n, v_new)
        state = state * gl_n[..., None, None] + jnp.einsum('bhcd,bhce->bhde', kd_n, v_new)
        return state, o_n

    xs = tuple(jnp.moveaxis(t, 2, 0) for t in (qg, w, u, a_intra, kd, g_last))
    s0 = jnp.zeros((B, H, DK, DV), jnp.float32)
    _, o = lax.scan(step, s0, xs)
    return jnp.transpose(o, (1, 0, 3, 2, 4)).reshape(B, S, H, DV)


def gdn_mixer(h, w_in, w_conv, a_log, dt_bias, g_norm, w_o):
    B, S, _ = h.shape
    proj = h @ w_in
    qkv, z, b_raw, a_raw = jnp.split(
        proj, [GDN_CONV_DIM, GDN_CONV_DIM + GDN_V_DIM, GDN_CONV_DIM + GDN_V_DIM + GDN_V_HEADS], axis=-1)
    qkv = jax.nn.silu(causal_depthwise_conv(qkv, w_conv))
    q, k, v = jnp.split(qkv, [GDN_Q_DIM, 2 * GDN_Q_DIM], axis=-1)
    rep = GDN_V_HEADS // GDN_QK_HEADS
    q = jnp.repeat(l2_normalize(q.reshape(B, S, GDN_QK_HEADS, GDN_DK).astype(jnp.float32)), rep, axis=2)
    k = jnp.repeat(l2_normalize(k.reshape(B, S, GDN_QK_HEADS, GDN_DK).astype(jnp.float32)), rep, axis=2)
    v = v.reshape(B, S, GDN_V_HEADS, GDN_DV).astype(jnp.float32)
    beta = jax.nn.sigmoid(b_raw.astype(jnp.float32))
    g = -jnp.exp(a_log.astype(jnp.float32)) * jax.nn.softplus(a_raw.astype(jnp.float32) + dt_bias)
    o = gated_delta_rule_chunked(q * GDN_DK ** -0.5, k, v, beta, g)
    o = rms_norm(o, g_norm) * jax.nn.silu(z.reshape(B, S, GDN_V_HEADS, GDN_DV).astype(jnp.float32))
    return o.reshape(B, S, GDN_V_DIM).astype(h.dtype) @ w_o


def clamped_swiglu(gu):
    gate, up = jnp.split(gu, 2, axis=-1)
    gate = jnp.minimum(gate, SWIGLU_LIMIT)
    up = jnp.clip(up, -SWIGLU_LIMIT, SWIGLU_LIMIT)
    return gate * jax.nn.sigmoid(SWIGLU_ALPHA * gate) * (up + 1.0)


def moe_ffn(h, w_router, b_router, w_gate_up, b_gate_up, w_down, b_down):
    B, S, D = h.shape
    T = B * S
    xt = h.reshape(T, D)
    logits = (xt @ w_router).astype(jnp.float32) + b_router
    top_vals, top_idx = lax.top_k(logits, TOP_K)
    gates = jax.nn.softmax(top_vals, axis=-1)
    n_assign = T * TOP_K
    e_flat = top_idx.reshape(-1)
    tok_flat = jnp.arange(n_assign, dtype=jnp.int32) // TOP_K
    gate_flat = gates.reshape(-1)
    order = jnp.argsort(e_flat)
    e_sorted = e_flat[order]
    counts = jnp.bincount(e_flat, length=N_EXPERTS)
    padded = (counts + EXPERT_BLOCK - 1) // EXPERT_BLOCK * EXPERT_BLOCK
    pend = jnp.cumsum(padded)
    pstart = pend - padded
    sstart = jnp.cumsum(counts) - counts
    dest = pstart[e_sorted] + jnp.arange(n_assign, dtype=jnp.int32) - sstart[e_sorted]
    n_rows = -(-n_assign // EXPERT_BLOCK) * EXPERT_BLOCK + N_EXPERTS * EXPERT_BLOCK
    n_blocks = n_rows // EXPERT_BLOCK
    row_tok = jnp.full((n_rows,), T, jnp.int32).at[dest].set(tok_flat[order])
    row_gate = jnp.zeros((n_rows,), jnp.float32).at[dest].set(gate_flat[order])
    blk_expert = jnp.minimum(
        jnp.searchsorted(pend, jnp.arange(n_blocks, dtype=pend.dtype) * EXPERT_BLOCK, side='right'),
        N_EXPERTS - 1)
    x_rows = jnp.concatenate([xt, jnp.zeros((1, D), xt.dtype)], axis=0)[row_tok]
    x_rows = x_rows.reshape(n_blocks, EXPERT_BLOCK, D)

    def expert_block(args):
        xb, e = args
        gu = xb @ w_gate_up[e] + b_gate_up[e]
        return clamped_swiglu(gu) @ w_down[e] + b_down[e]

    y = lax.map(expert_block, (x_rows, blk_expert)).reshape(n_rows, D)
    out = jnp.zeros((T + 1, D), jnp.float32).at[row_tok].add(y.astype(jnp.float32) * row_gate[:, None])
    return out[:T].reshape(B, S, D).astype(h.dtype)


def setup_inputs(seed: int = 0) -> dict:
    key = jax.random.key(seed)
    ks = iter(jax.random.split(key, 40))

    def nrm(shape, s):
        return jax.random.normal(next(ks), shape, jnp.float32) * s

    D = D_MODEL
    dt = jnp.exp(jax.random.uniform(next(ks), (N_LAYERS_C, GDN_V_HEADS), jnp.float32,
                                    math.log(1e-3), math.log(1e-1)))
    return {
        "x": nrm((BATCH, SEQ, D), 1.0),
        "c": nrm((BATCH, D), 1.0),
        "ada_w": nrm((DEPTH, D, 6 * D), 0.25 * D ** -0.5),
        "ada_b": nrm((DEPTH, 6 * D), 0.01),
        "ln_g": 1.0 + nrm((DEPTH, 2, D), 0.05),
        "ln_b": nrm((DEPTH, 2, D), 0.02),
        "mla_w_dqkv": nrm((N_LAYERS_A, D, MLA_DOWN), D ** -0.5),
        "mla_g_q": 1.0 + nrm((N_LAYERS_A, MLA_Q_LORA), 0.05),
        "mla_g_kv": 1.0 + nrm((N_LAYERS_A, MLA_KV_LORA), 0.05),
        "mla_w_uq": nrm((N_LAYERS_A, MLA_Q_LORA, MLA_HEADS * (MLA_NOPE + MLA_ROPE)), MLA_Q_LORA ** -0.5),
        "mla_w_ukv": nrm((N_LAYERS_A, MLA_KV_LORA, MLA_HEADS * (MLA_NOPE + MLA_V)), MLA_KV_LORA ** -0.5),
        "mla_w_o": nrm((N_LAYERS_A, MLA_HEADS * MLA_V, D), (MLA_HEADS * MLA_V) ** -0.5 * DEEPNORM_BETA),
        "pool_w": nrm((N_LAYERS_B, POOL_GROUPS, POOL_GROUP_DIM, POOL_GROUP_DIM),
                       POOL_GROUP_DIM ** -0.5 * DEEPNORM_BETA),
        "pool_scale": 1.0 + nrm((N_LAYERS_B, D), 0.05),
        "gdn_w_in": nrm((N_LAYERS_C, D, GDN_PROJ_DIM), D ** -0.5),
        "gdn_w_conv": nrm((N_LAYERS_C, GDN_CONV, GDN_CONV_DIM), GDN_CONV ** -0.5),
        "gdn_a_log": jnp.log(jax.random.uniform(next(ks), (N_LAYERS_C, GDN_V_HEADS), jnp.float32, 1.0, 16.0)),
        "gdn_dt_bias": dt + jnp.log(-jnp.expm1(-dt)),
        "gdn_g_norm": 1.0 + nrm((N_LAYERS_C, GDN_DV), 0.05),
        "gdn_w_o": nrm((N_LAYERS_C, GDN_V_DIM, D), GDN_V_DIM ** -0.5 * DEEPNORM_BETA),
        "moe_w_router": nrm((DEPTH, D, N_EXPERTS), D ** -0.5),
        "moe_b_router": nrm((DEPTH, N_EXPERTS), 0.01),
        "moe_w_gate_up": nrm((DEPTH, N_EXPERTS, D, 2 * D_EXPERT), D ** -0.5),
        "moe_b_gate_up": nrm((DEPTH, N_EXPERTS, 2 * D_EXPERT), 0.01),
        "moe_w_down": nrm((DEPTH, N_EXPERTS, D_EXPERT, D), D_EXPERT ** -0.5 * DEEPNORM_BETA),
        "moe_b_down": nrm((DEPTH, N_EXPERTS, D), 0.01),
    }


def reference(x, c, ada_w, ada_b, ln_g, ln_b,
              mla_w_dqkv, mla_g_q, mla_g_kv, mla_w_uq, mla_w_ukv, mla_w_o,
              pool_w, pool_scale,
              gdn_w_in, gdn_w_conv, gdn_a_log, gdn_dt_bias, gdn_g_norm, gdn_w_o,
              moe_w_router, moe_b_router, moe_w_gate_up, moe_b_gate_up, moe_w_down, moe_b_down):
    S = x.shape[1]
    cos, sin = rope_tables(S)
    mods = jnp.einsum('bd,lde->lbe', jax.nn.silu(c), ada_w) + ada_b[:, None, :]
    for i in range(DEPTH):
        shift_t, scale_t, gate_t, shift_f, scale_f, gate_f = jnp.split(mods[i][:, None, :], 6, axis=-1)
        kind, j = i % N_MIXERS, i // N_MIXERS
        h = x * (1.0 + scale_t) + shift_t
        if kind == 0:
            y = mla_mixer(h, cos, sin, mla_w_dqkv[j], mla_g_q[j], mla_g_kv[j], mla_w_uq[j], mla_w_ukv[j], mla_w_o[j])
        elif kind == 1:
            y = pool_mixer(h, pool_w[j], pool_scale[j])
        else:
            y = gdn_mixer(h, gdn_w_in[j], gdn_w_conv[j], gdn_a_log[j], gdn_dt_bias[j], gdn_g_norm[j], gdn_w_o[j])
        x = layer_norm(DEEPNORM_ALPHA * x + (1.0 + gate_t) * y, ln_g[i, 0], ln_b[i, 0])
        h = x * (1.0 + scale_f) + shift_f
        y = moe_ffn(h, moe_w_router[i], moe_b_router[i], moe_w_gate_up[i], moe_b_gate_up[i],
                    moe_w_down[i], moe_b_down[i])
        x = layer_norm(DEEPNORM_ALPHA * x + (1.0 + gate_f) * y, ln_g[i, 1], ln_b[i, 1])
    return x
```

```python
import functools
import math

import jax
import jax.numpy as jnp
from jax import lax
from jax.experimental import pallas as pl
from jax.experimental.pallas import tpu as pltpu

F32 = jnp.float32
BF16 = jnp.bfloat16

CHUNK = 64
LN_EPS = 1e-5
RMS_EPS = 1e-6

MLA_HEADS = 8
MLA_NOPE = 128
MLA_ROPE = 64
MLA_V = 128
MLA_Q_LORA = 384
MLA_KV_LORA = 256
ROPE_THETA = 10000.0
MLA_QK = 256

POOL_WINDOWS = (2, 4, 8, 16)
POOL_HALO = 16

GDN_QK_HEADS = 8
GDN_V_HEADS = 16
GDN_DK = 128
GDN_DV = 128
GDN_CONV = 4
GDN_Q_DIM = GDN_QK_HEADS * GDN_DK
GDN_V_DIM = GDN_V_HEADS * GDN_DV
GDN_CONV_DIM = 2 * GDN_Q_DIM + GDN_V_DIM
GDN_HALO = 8

N_EXPERTS = 32
TOP_K = 4
SWIGLU_LIMIT = 7.0
SWIGLU_ALPHA = 1.702

VMEM_LIMIT = 52 * 1024 * 1024


def _cparams(sem):
    return pltpu.CompilerParams(dimension_semantics=sem, vmem_limit_bytes=VMEM_LIMIT)


def _bdot(a, b):
    return jnp.dot(a.astype(BF16), b.astype(BF16), preferred_element_type=F32)


def _bdot_nt(a, b):
    return lax.dot_general(a.astype(BF16), b.astype(BF16), (((1,), (1,)), ((), ())),
                           preferred_element_type=F32)


def _bdot_tn(a, b):
    return lax.dot_general(a.astype(BF16), b.astype(BF16), (((0,), (0,)), ((), ())),
                           preferred_element_type=F32)


def _split(a):
    hi = a.astype(BF16)
    lo = (a - hi.astype(F32)).astype(BF16)
    return hi, lo


def _dot3(a, b):
    ah, al = _split(a)
    bh, bl = _split(b)
    d = functools.partial(jnp.dot, preferred_element_type=F32)
    return d(ah, bh) + (d(ah, bl) + d(al, bh))


def _dot3_nt(a, b):
    ah, al = _split(a)
    bh, bl = _split(b)
    d = lambda x, y: lax.dot_general(x, y, (((1,), (1,)), ((), ())), preferred_element_type=F32)
    return d(ah, bh) + (d(ah, bl) + d(al, bh))


def _layer_norm(z, g, b):
    mu = jnp.mean(z, axis=-1, keepdims=True)
    zc = z - mu
    var = jnp.mean(zc * zc, axis=-1, keepdims=True)
    return zc * lax.rsqrt(var + LN_EPS) * g + b


def _sigmoid(x):
    return 1.0 / (1.0 + jnp.exp(-x))


def _silu(x):
    return x * _sigmoid(x)


def _ada_kernel(c_ref, w_ref, b_ref, o_ref):
    sc = _silu(c_ref[...])
    o_ref[...] = _dot3(sc, w_ref[...]) + b_ref[...]


def _ada_mods(c, ada_w, ada_b):
    depth, d, n = ada_w.shape
    bsz = c.shape[0]
    rows = 8
    tn = 2048
    c8 = jnp.zeros((rows, d), F32).at[:bsz].set(c)
    out = pl.pallas_call(
        _ada_kernel,
        grid=(depth, n // tn),
        in_specs=[
            pl.BlockSpec((rows, d), lambda l, j: (0, 0)),
            pl.BlockSpec((None, d, tn), lambda l, j: (l, 0, j)),
            pl.BlockSpec((None, 1, tn), lambda l, j: (l, 0, j)),
        ],
        out_specs=pl.BlockSpec((None, rows, tn), lambda l, j: (l, 0, j)),
        out_shape=jax.ShapeDtypeStruct((depth, rows, n), F32),
        compiler_params=_cparams(("arbitrary", "arbitrary")),
        name="ada_mods",
    )(c8, ada_w, ada_b.reshape(depth, 1, n))
    return out[:, :bsz].reshape(depth, bsz, 6, d)


def _post_kernel(alpha, gate_row, y_ref, w_ref, x_ref, mod_ref, g_ref, b_ref, o_ref):
    y = jnp.dot(y_ref[...], w_ref[...], preferred_element_type=F32)
    gate = mod_ref[gate_row:gate_row + 1, :]
    z = alpha * x_ref[...] + (1.0 + gate) * y
    o_ref[...] = _layer_norm(z, g_ref[...], b_ref[...])


def _post(y_pre, w_bf, x, mods_i, ln_g, ln_b, alpha, gate_row, tm=512):
    bsz, s, d = x.shape
    kd = y_pre.shape[-1]
    return pl.pallas_call(
        functools.partial(_post_kernel, alpha, gate_row),
        grid=(bsz, s // tm),
        in_specs=[
            pl.BlockSpec((None, tm, kd), lambda b, i: (b, i, 0)),
            pl.BlockSpec((kd, d), lambda b, i: (0, 0)),
            pl.BlockSpec((None, tm, d), lambda b, i: (b, i, 0)),
            pl.BlockSpec((None, 6, d), lambda b, i: (b, 0, 0)),
            pl.BlockSpec((1, d), lambda b, i: (0, 0)),
            pl.BlockSpec((1, d), lambda b, i: (0, 0)),
        ],
        out_specs=pl.BlockSpec((None, tm, d), lambda b, i: (b, i, 0)),
        out_shape=jax.ShapeDtypeStruct((bsz, s, d), F32),
        compiler_params=_cparams(("arbitrary", "arbitrary")),
        name="post_proj_ln",
    )(y_pre, w_bf, x, mods_i, ln_g.reshape(1, d), ln_b.reshape(1, d))


def _mla_proj_kernel(qscale, x_ref, mod_ref, wd_ref, gq_ref, gkv_ref, wuq_ref, wukv_ref, cos_ref, sin_ref,
                     q_ref, k_ref, v_ref):
    shift = mod_ref[0:1, :]
    scale = mod_ref[1:2, :]
    h = x_ref[...] * (1.0 + scale) + shift
    down = _bdot(h, wd_ref[...])
    cq = down[:, :MLA_Q_LORA]
    ckv = down[:, MLA_Q_LORA:MLA_Q_LORA + MLA_KV_LORA]
    kra = down[:, 640:768]
    krb = down[:, 768:896]
    cos = cos_ref[...]
    sin = sin_ref[...]
    k_rope = (kra * cos + krb * sin).astype(BF16)

    cqn = cq * lax.rsqrt(jnp.mean(cq * cq, axis=-1, keepdims=True) + RMS_EPS) * gq_ref[...]
    ckvn = ckv * lax.rsqrt(jnp.mean(ckv * ckv, axis=-1, keepdims=True) + RMS_EPS) * gkv_ref[...]
    cqn = cqn.astype(BF16)
    ckvn = ckvn.astype(BF16)
    nh = MLA_HEADS
    for hh in range(nh):
        qn = jnp.dot(cqn, wuq_ref[:, hh * 128:(hh + 1) * 128], preferred_element_type=F32)
        qa = jnp.dot(cqn, wuq_ref[:, (nh + hh) * 128:(nh + hh + 1) * 128], preferred_element_type=F32)
        qb = jnp.dot(cqn, wuq_ref[:, (2 * nh + hh) * 128:(2 * nh + hh + 1) * 128], preferred_element_type=F32)
        q_ref[hh, :, 0:128] = (qn * qscale).astype(BF16)
        q_ref[hh, :, 128:256] = ((qa * cos + qb * sin) * qscale).astype(BF16)
        kv = jnp.dot(ckvn, wukv_ref[:, hh * 256:(hh + 1) * 256], preferred_element_type=F32)
        k_ref[hh, :, 0:128] = kv[:, :128].astype(BF16)
        k_ref[hh, :, 128:256] = k_rope
        v_ref[hh, :, :] = kv[:, 128:].astype(BF16)


def _mla_attn_kernel(tq, tk, q_ref, k_ref, v_ref, o_ref):
    i = pl.program_id(2)
    q = q_ref[...]

    def block(j, carry, masked):
        m, l, acc = carry
        start = pl.multiple_of(j * tk, tk)
        kb = k_ref[pl.ds(start, tk), :]
        vb = v_ref[pl.ds(start, tk), :]
        s = lax.dot_general(q, kb, (((1,), (1,)), ((), ())), preferred_element_type=F32)
        if masked:
            qc = (i * tq + lax.broadcasted_iota(jnp.int32, (tq, tk), 0)) // CHUNK
            kc = (j * tk + lax.broadcasted_iota(jnp.int32, (tq, tk), 1)) // CHUNK
            s = jnp.where(kc <= qc, s, -1e30)
        m_new = jnp.maximum(m, jnp.max(s, axis=-1, keepdims=True))
        p = jnp.exp(s - m_new)
        corr = jnp.exp(m - m_new)
        l_new = corr * l + jnp.sum(p, axis=-1, keepdims=True)
        acc_new = corr * acc + jnp.dot(p.astype(BF16), vb, preferred_element_type=F32)
        return m_new, l_new, acc_new

    init = (jnp.full((tq, 1), -1e30, F32), jnp.zeros((tq, 1), F32), jnp.zeros((tq, MLA_V), F32))
    carry = lax.fori_loop(0, i, lambda j, c: block(j, c, False), init)
    m, l, acc = block(i, carry, True)
    o_ref[...] = (acc / l).astype(BF16)


def _mla_layer(x, mods_i, w_dqkv, g_q, g_kv, w_uq, w_ukv, w_o, ln_g, ln_b, alpha, tm=512, tq=512):
    bsz, s, d = x.shape
    nh = MLA_HEADS
    half = MLA_ROPE // 2
    k1 = w_dqkv[:, 640:640 + half]
    k2 = w_dqkv[:, 640 + half:704]
    z64 = jnp.zeros((d, 64), F32)
    wd = jnp.concatenate([w_dqkv[:, :640], k1, k2, z64, k2, k1, z64], axis=1).astype(BF16)
    wq = w_uq.reshape(MLA_Q_LORA, nh, MLA_NOPE + MLA_ROPE)
    wq_n = wq[:, :, :MLA_NOPE].reshape(MLA_Q_LORA, nh * 128)
    x1 = wq[:, :, MLA_NOPE:MLA_NOPE + half]
    x2 = wq[:, :, MLA_NOPE + half:]
    zq = jnp.zeros((MLA_Q_LORA, nh, 64), F32)
    wq_a = jnp.concatenate([x1, x2, zq], axis=2).reshape(MLA_Q_LORA, nh * 128)
    wq_b = jnp.concatenate([x2, x1, zq], axis=2).reshape(MLA_Q_LORA, nh * 128)
    wuq = jnp.concatenate([wq_n, wq_a, wq_b], axis=1).astype(BF16)
    wukv = w_ukv.astype(BF16)
    pos = jnp.arange(s, dtype=F32)
    inv_freq = ROPE_THETA ** (-jnp.arange(0, MLA_ROPE, 2, dtype=F32) / MLA_ROPE)
    ang = pos[:, None] * inv_freq[None, :]
    cs, sn = jnp.cos(ang), jnp.sin(ang)
    zt = jnp.zeros((s, 64), F32)
    cos_t = jnp.concatenate([cs, cs, zt], axis=1)
    sin_t = jnp.concatenate([-sn, sn, zt], axis=1)
    qscale = (MLA_NOPE + MLA_ROPE) ** -0.5

    q, k, v = pl.pallas_call(
        functools.partial(_mla_proj_kernel, qscale),
        grid=(bsz, s // tm),
        in_specs=[
            pl.BlockSpec((None, tm, d), lambda b, i: (b, i, 0)),
            pl.BlockSpec((None, 6, d), lambda b, i: (b, 0, 0)),
            pl.BlockSpec(wd.shape, lambda b, i: (0, 0)),
            pl.BlockSpec((1, MLA_Q_LORA), lambda b, i: (0, 0)),
            pl.BlockSpec((1, MLA_KV_LORA), lambda b, i: (0, 0)),
            pl.BlockSpec(wuq.shape, lambda b, i: (0, 0)),
            pl.BlockSpec(wukv.shape, lambda b, i: (0, 0)),
            pl.BlockSpec((tm, 128), lambda b, i: (i, 0)),
            pl.BlockSpec((tm, 128), lambda b, i: (i, 0)),
        ],
        out_specs=[
            pl.BlockSpec((None, nh, tm, MLA_QK), lambda b, i: (b, 0, i, 0)),
            pl.BlockSpec((None, nh, tm, MLA_QK), lambda b, i: (b, 0, i, 0)),
            pl.BlockSpec((None, nh, tm, MLA_V), lambda b, i: (b, 0, i, 0)),
        ],
        out_shape=[
            jax.ShapeDtypeStruct((bsz, nh, s, MLA_QK), BF16),
            jax.ShapeDtypeStruct((bsz, nh, s, MLA_QK), BF16),
            jax.ShapeDtypeStruct((bsz, nh, s, MLA_V), BF16),
        ],
        compiler_params=_cparams(("arbitrary", "arbitrary")),
        name="mla_proj",
    )(x, mods_i, wd, g_q.reshape(1, -1), g_kv.reshape(1, -1), wuq, wukv, cos_t, sin_t)

    o = pl.pallas_call(
        functools.partial(_mla_attn_kernel, tq, tq),
        grid=(bsz, nh, s // tq),
        in_specs=[
            pl.BlockSpec((None, None, tq, MLA_QK), lambda b, h, i: (b, h, i, 0)),
            pl.BlockSpec((None, None, s, MLA_QK), lambda b, h, i: (b, h, 0, 0)),
            pl.BlockSpec((None, None, s, MLA_V), lambda b, h, i: (b, h, 0, 0)),
        ],
        out_specs=pl.BlockSpec((None, tq, MLA_V), lambda b, h, i: (b, i, h)),
        out_shape=jax.ShapeDtypeStruct((bsz, s, nh * MLA_V), BF16),
        compiler_params=_cparams(("arbitrary", "arbitrary", "arbitrary")),
        name="mla_attn",
    )(q, k, v)

    return _post(o, w_o.astype(BF16), x, mods_i, ln_g, ln_b, alpha, gate_row=2)


def _pool_kernel(alpha, tm, x_ref, halo_ref, mod_ref, w_ref, cs_ref, g_ref, b_ref, o_ref, buf):
    i = pl.program_id(1)
    d = x_ref.shape[-1]
    gd = d // len(POOL_WINDOWS)
    shift = mod_ref[0:1, :]
    scale = mod_ref[1:2, :]
    gate = mod_ref[2:3, :]
    x = x_ref[...]
    h = x * (1.0 + scale) + shift
    hh = halo_ref[...] * (1.0 + scale) + shift
    hh = jnp.where(i > 0, hh, 0.0)
    buf[0:POOL_HALO, :] = hh
    buf[POOL_HALO:, :] = h
    t = i * tm + lax.broadcasted_iota(jnp.int32, (tm, 1), 0)
    ys = []
    for g, win in enumerate(POOL_WINDOWS):
        cols = slice(g * gd, (g + 1) * gd)
        acc = buf[POOL_HALO:POOL_HALO + tm, cols]
        for j in range(1, win):
            acc = acc + buf[POOL_HALO - j:POOL_HALO - j + tm, cols]
        count = jnp.minimum(t + 1, win).astype(F32)
        dg = acc / count - h[:, cols]
        ys.append(_bdot(dg, w_ref[g]))
    y = jnp.concatenate(ys, axis=-1) * cs_ref[...]
    z = alpha * x + (1.0 + gate) * y
    o_ref[...] = _layer_norm(z, g_ref[...], b_ref[...])


def _pool_layer(x, mods_i, w_pool, ch_scale, ln_g, ln_b, alpha, tm=512):
    bsz, s, d = x.shape
    gd = d // len(POOL_WINDOWS)
    hb = tm // POOL_HALO
    return pl.pallas_call(
        functools.partial(_pool_kernel, alpha, tm),
        grid=(bsz, s // tm),
        in_specs=[
            pl.BlockSpec((None, tm, d), lambda b, i: (b, i, 0)),
            pl.BlockSpec((None, POOL_HALO, d), lambda b, i: (b, jnp.maximum(i * hb - 1, 0), 0)),
            pl.BlockSpec((None, 6, d), lambda b, i: (b, 0, 0)),
            pl.BlockSpec((len(POOL_WINDOWS), gd, gd), lambda b, i: (0, 0, 0)),
            pl.BlockSpec((1, d), lambda b, i: (0, 0)),
            pl.BlockSpec((1, d), lambda b, i: (0, 0)),
            pl.BlockSpec((1, d), lambda b, i: (0, 0)),
        ],
        out_specs=pl.BlockSpec((None, tm, d), lambda b, i: (b, i, 0)),
        out_shape=jax.ShapeDtypeStruct((bsz, s, d), F32),
        scratch_shapes=[pltpu.VMEM((tm + POOL_HALO, d), F32)],
        compiler_params=_cparams(("arbitrary", "arbitrary")),
        name="pool_layer",
    )(x, x, mods_i, w_pool.astype(BF16), ch_scale.reshape(1, d), ln_g.reshape(1, d), ln_b.reshape(1, d))


def _gdn_proj_kernel(tm, x_ref, halo_ref, mod_ref, wqkv_ref, wz_ref, wba_ref, wc_ref, alog_ref, dtb_ref,
                     q_ref, k_ref, v_ref, z_ref, beta_ref, gc_ref, hbuf, pbuf):
    i = pl.program_id(1)
    shift = mod_ref[0:1, :]
    scale = mod_ref[1:2, :]
    h = x_ref[...] * (1.0 + scale) + shift
    hh = halo_ref[...] * (1.0 + scale) + shift
    hh = jnp.where(i > 0, hh, 0.0)
    hbuf[0:GDN_HALO, :] = hh
    hbuf[GDN_HALO:, :] = h
    h_ext = hbuf[...].astype(BF16)

    z_ref[...] = jnp.dot(h.astype(BF16), wz_ref[...], preferred_element_type=F32).astype(BF16)

    ba = _dot3(h, wba_ref[...])
    nv = GDN_V_HEADS
    beta_ref[...] = _sigmoid(ba[:, :nv])
    a = ba[:, nv:] + dtb_ref[...]
    softplus = jnp.maximum(a, 0.0) + jnp.log(1.0 + jnp.exp(-jnp.abs(a)))
    g = -jnp.exp(alog_ref[...]) * softplus
    r = lax.broadcasted_iota(jnp.int32, (tm, tm), 0)
    c = lax.broadcasted_iota(jnp.int32, (tm, tm), 1)
    tri = jnp.where((c <= r) & (c // CHUNK == r // CHUNK), 1.0, 0.0).astype(BF16)
    g_hi = g.astype(BF16)
    g_lo = (g - g_hi.astype(F32)).astype(BF16)
    g_lo2 = (g - g_hi.astype(F32) - g_lo.astype(F32)).astype(BF16)
    d = functools.partial(jnp.dot, preferred_element_type=F32)
    gc_ref[...] = d(tri, g_hi) + (d(tri, g_lo) + d(tri, g_lo2))

    cw = 512
    for cb in range(GDN_CONV_DIM // cw):
        cols = slice(cb * cw, (cb + 1) * cw)
        pbuf[...] = jnp.dot(h_ext, wqkv_ref[:, cols], preferred_element_type=F32)
        acc = pbuf[GDN_HALO:GDN_HALO + tm, :] * wc_ref[GDN_CONV - 1:GDN_CONV, cols]
        for j in range(GDN_CONV - 1):
            off = GDN_HALO - (GDN_CONV - 1) + j
            acc = acc + pbuf[off:off + tm, :] * wc_ref[j:j + 1, cols]
        y = _silu(acc)
        if cb * cw < 2 * GDN_Q_DIM:
            outs = []
            for hd in range(cw // GDN_DK):
                yh = y[:, hd * GDN_DK:(hd + 1) * GDN_DK]
                outs.append(yh * lax.rsqrt(jnp.sum(yh * yh, axis=-1, keepdims=True) + RMS_EPS))
            y = jnp.concatenate(outs, axis=-1)
            if cb * cw < GDN_Q_DIM:
                q_ref[:, cols] = y * (GDN_DK ** -0.5)
            else:
                k_ref[:, cb * cw - GDN_Q_DIM:(cb + 1) * cw - GDN_Q_DIM] = y
        else:
            v_ref[:, cb * cw - 2 * GDN_Q_DIM:(cb + 1) * cw - 2 * GDN_Q_DIM] = y


def _col(x, hidx):
    lane = lax.broadcasted_iota(jnp.int32, x.shape, 1)
    return jnp.sum(jnp.where(lane == hidx, x, 0.0), axis=-1, keepdims=True)


def _gdn_chunk_kernel(tb, q_ref, k_ref, v_ref, z_ref, beta_ref, gc_ref, gr_ref, gn_ref, o_ref, state):
    hv = pl.program_id(1)
    n = pl.program_id(2)

    @pl.when(n == 0)
    def _():
        state[...] = jnp.zeros_like(state)

    beta_all = _col(beta_ref[...], hv)
    gcol_all = _col(gc_ref[...], hv)
    ri = lax.broadcasted_iota(jnp.int32, (CHUNK, CHUNK), 0)
    ci = lax.broadcasted_iota(jnp.int32, (CHUNK, CHUNK), 1)
    eye = jnp.where(ri == ci, 1.0, 0.0).astype(F32)
    s_cur = state[...]
    for c in range(tb // CHUNK):
        rows = slice(c * CHUNK, (c + 1) * CHUNK)
        q = q_ref[rows, :]
        k = k_ref[rows, :]
        v = v_ref[rows, :]
        beta = beta_all[rows, :]
        gcol = gcol_all[rows, :]
        grow = gr_ref[c:c + 1, :]
        diff = gcol - grow
        decay = jnp.exp(jnp.where(ri >= ci, diff, -jnp.inf))
        eg = jnp.exp(gcol)
        glast = gcol[CHUNK - 1:CHUNK, :]
        kb = k * beta
        vb = v * beta
        kk = _dot3_nt(kb, k)
        xm = -jnp.where(ri > ci, kk * decay, 0.0)
        tinv = eye + xm
        pw = xm
        for _ in range(5):
            pw = _dot3(pw, pw)
            tinv = tinv + _dot3(tinv, pw)
        u = _bdot(tinv, vb)
        w = _bdot(tinv, kb * eg)
        a_intra = _bdot_nt(q, k) * decay
        kd = k * jnp.exp(glast - gcol)
        v_new = u - _bdot(w, s_cur)
        o = _bdot(q * eg, s_cur) + _bdot(a_intra, v_new)
        s_cur = s_cur * jnp.exp(glast) + _bdot_tn(kd, v_new)
        on = o * lax.rsqrt(jnp.mean(o * o, axis=-1, keepdims=True) + RMS_EPS) * gn_ref[...]
        zz = z_ref[rows, :].astype(F32)
        o_ref[rows, :] = (on * _silu(zz)).astype(BF16)
    state[...] = s_cur


def _gdn_layer(x, mods_i, w_in, w_conv, a_log, dt_bias, g_norm, w_o, ln_g, ln_b, alpha, tm=256, tb=512):
    bsz, s, d = x.shape
    nv = GDN_V_HEADS
    wqkv = w_in[:, :GDN_CONV_DIM].astype(BF16)
    wz = w_in[:, GDN_CONV_DIM:GDN_CONV_DIM + GDN_V_DIM].astype(BF16)
    wba = w_in[:, GDN_CONV_DIM + GDN_V_DIM:]
    hb = tm // GDN_HALO
    q, k, v, z, beta, gc = pl.pallas_call(
        functools.partial(_gdn_proj_kernel, tm),
        grid=(bsz, s // tm),
        in_specs=[
            pl.BlockSpec((None, tm, d), lambda b, i: (b, i, 0)),
            pl.BlockSpec((None, GDN_HALO, d), lambda b, i: (b, jnp.maximum(i * hb - 1, 0), 0)),
            pl.BlockSpec((None, 6, d), lambda b, i: (b, 0, 0)),
            pl.BlockSpec(wqkv.shape, lambda b, i: (0, 0)),
            pl.BlockSpec(wz.shape, lambda b, i: (0, 0)),
            pl.BlockSpec(wba.shape, lambda b, i: (0, 0)),
            pl.BlockSpec((GDN_CONV, GDN_CONV_DIM), lambda b, i: (0, 0)),
            pl.BlockSpec((1, nv), lambda b, i: (0, 0)),
            pl.BlockSpec((1, nv), lambda b, i: (0, 0)),
        ],
        out_specs=[
            pl.BlockSpec((None, tm, GDN_Q_DIM), lambda b, i: (b, i, 0)),
            pl.BlockSpec((None, tm, GDN_Q_DIM), lambda b, i: (b, i, 0)),
            pl.BlockSpec((None, tm, GDN_V_DIM), lambda b, i: (b, i, 0)),
            pl.BlockSpec((None, tm, GDN_V_DIM), lambda b, i: (b, i, 0)),
            pl.BlockSpec((None, tm, nv), lambda b, i: (b, i, 0)),
            pl.BlockSpec((None, tm, nv), lambda b, i: (b, i, 0)),
        ],
        out_shape=[
            jax.ShapeDtypeStruct((bsz, s, GDN_Q_DIM), F32),
            jax.ShapeDtypeStruct((bsz, s, GDN_Q_DIM), F32),
            jax.ShapeDtypeStruct((bsz, s, GDN_V_DIM), F32),
            jax.ShapeDtypeStruct((bsz, s, GDN_V_DIM), BF16),
            jax.ShapeDtypeStruct((bsz, s, nv), F32),
            jax.ShapeDtypeStruct((bsz, s, nv), F32),
        ],
        scratch_shapes=[pltpu.VMEM((tm + GDN_HALO, d), F32), pltpu.VMEM((tm + GDN_HALO, 512), F32)],
        compiler_params=_cparams(("arbitrary", "arbitrary")),
        name="gdn_proj",
    )(x, x, mods_i, wqkv, wz, wba, w_conv, a_log.reshape(1, nv), dt_bias.reshape(1, nv))

    gr = gc.transpose(0, 2, 1).reshape(bsz, nv, s // CHUNK, CHUNK)
    rep = GDN_V_HEADS // GDN_QK_HEADS
    cpb = tb // CHUNK
    o = pl.pallas_call(
        functools.partial(_gdn_chunk_kernel, tb),
        grid=(bsz, nv, s // tb),
        in_specs=[
            pl.BlockSpec((None, tb, GDN_DK), lambda b, h, n: (b, n, h // rep)),
            pl.BlockSpec((None, tb, GDN_DK), lambda b, h, n: (b, n, h // rep)),
            pl.BlockSpec((None, tb, GDN_DV), lambda b, h, n: (b, n, h)),
            pl.BlockSpec((None, tb, GDN_DV), lambda b, h, n: (b, n, h)),
            pl.BlockSpec((None, tb, nv), lambda b, h, n: (b, n, 0)),
            pl.BlockSpec((None, tb, nv), lambda b, h, n: (b, n, 0)),
            pl.BlockSpec((None, None, cpb, CHUNK), lambda b, h, n: (b, h, n, 0)),
            pl.BlockSpec((1, GDN_DV), lambda b, h, n: (0, 0)),
        ],
        out_specs=pl.BlockSpec((None, tb, GDN_DV), lambda b, h, n: (b, n, h)),
        out_shape=jax.ShapeDtypeStruct((bsz, s, GDN_V_DIM), BF16),
        scratch_shapes=[pltpu.VMEM((GDN_DK, GDN_DV), F32)],
        compiler_params=_cparams(("arbitrary", "arbitrary", "arbitrary")),
        name="gdn_chunk",
    )(q, k, v, z, beta, gc, gr, g_norm.reshape(1, GDN_DV))

    return _post(o, w_o.astype(BF16), x, mods_i, ln_g, ln_b, alpha, gate_row=2)


def _router_kernel(tm, x_ref, mod_ref, wr_ref, br_ref, h_ref, idx_ref, gate_ref, rank_ref, cnt_ref, run):
    step = pl.program_id(0) * pl.num_programs(1) + pl.program_id(1)

    @pl.when(step == 0)
    def _():
        run[...] = jnp.zeros_like(run)

    shift = mod_ref[3:4, :]
    scale = mod_ref[4:5, :]
    h = x_ref[...] * (1.0 + scale) + shift
    h_ref[...] = h
    logits = _dot3_nt(wr_ref[...], h) + br_ref[...]
    eid = lax.broadcasted_iota(jnp.int32, (N_EXPERTS, tm), 0)
    vals, idxs, hots = [], [], []
    cur = logits
    for _ in range(TOP_K):
        m = jnp.max(cur, axis=0, keepdims=True)
        sel = jnp.min(jnp.where(cur == m, eid, N_EXPERTS), axis=0, keepdims=True)
        hot = eid == sel
        vals.append(m)
        idxs.append(sel)
        hots.append(hot)
        cur = jnp.where(hot, -jnp.inf, cur)
    es = [jnp.exp(v - vals[0]) for v in vals]
    den = es[0] + es[1] + es[2] + es[3]
    cnt = jnp.zeros((N_EXPERTS, tm), F32)
    for hot in hots:
        cnt = cnt + jnp.where(hot, 1.0, 0.0)
    r = lax.broadcasted_iota(jnp.int32, (tm, tm), 0)
    c = lax.broadcasted_iota(jnp.int32, (tm, tm), 1)
    before = jnp.where(r < c, 1.0, 0.0).astype(BF16)
    prefix = jnp.dot(cnt.astype(BF16), before, preferred_element_type=F32) + run[:, 0:1]
    for kk in range(TOP_K):
        idx_ref[kk:kk + 1, :] = idxs[kk]
        gate_ref[kk:kk + 1, :] = es[kk] / den
        rank_ref[kk:kk + 1, :] = jnp.sum(jnp.where(hots[kk], prefix, 0.0), axis=0, keepdims=True).astype(jnp.int32)
    run[...] = run[...] + jnp.sum(cnt, axis=1, keepdims=True)
    cnt_ref[...] = run[...]


def _expert_kernel(tm, be_ref, nu_ref, tok_hbm, h_hbm, wgu_ref, bgu_ref, wd_ref, bd_ref, y_ref,
                   tok_smem, xbuf, wgu_bf, wd_bf, tok_sem, x_sem):
    j = pl.program_id(0)
    n_used = nu_ref[0]
    slot = j % 2
    nxt = 1 - slot

    def tok_copy(blk, sl):
        return pltpu.make_async_copy(tok_hbm.at[blk], tok_smem.at[sl], tok_sem.at[sl])

    def issue_rows(sl):
        def body(r, carry):
            t = tok_smem[sl, r]
            pltpu.make_async_copy(h_hbm.at[pl.ds(t, 1), :], xbuf.at[sl, pl.ds(r, 1), :], x_sem.at[sl]).start()
            return carry
        lax.fori_loop(0, tm, body, 0, unroll=8)

    @pl.when(j == 0)
    def _():
        tok_copy(0, 0).start()
        tok_copy(0, 0).wait()
        issue_rows(0)

        @pl.when(n_used > 1)
        def _():
            tok_copy(1, 1).start()

    @pl.when(j + 1 < n_used)
    def _():
        tok_copy(j + 1, nxt).wait()
        issue_rows(nxt)

    @pl.when(j + 2 < n_used)
    def _():
        tok_copy(j + 2, slot).start()

    changed = jnp.logical_or(j == 0, be_ref[j] != be_ref[jnp.maximum(j - 1, 0)])

    @pl.when(jnp.logical_and(changed, j < n_used))
    def _():
        wgu_bf[...] = wgu_ref[...].astype(BF16)
        wd_bf[...] = wd_ref[...].astype(BF16)

    @pl.when(j < n_used)
    def _():
        pltpu.make_async_copy(h_hbm.at[pl.ds(0, tm), :], xbuf.at[slot], x_sem.at[slot]).wait()
        xb = xbuf[slot].astype(BF16)
        gu = jnp.dot(xb, wgu_bf[...], preferred_element_type=F32) + bgu_ref[...]
        dh = gu.shape[-1] // 2
        gate = jnp.minimum(gu[:, :dh], SWIGLU_LIMIT)
        up = jnp.clip(gu[:, dh:], -SWIGLU_LIMIT, SWIGLU_LIMIT)
        act = gate * _sigmoid(SWIGLU_ALPHA * gate) * (up + 1.0)
        y_ref[...] = jnp.dot(act.astype(BF16), wd_bf[...], preferred_element_type=F32) + bd_ref[...]

    @pl.when(j >= n_used)
    def _():
        y_ref[...] = jnp.zeros_like(y_ref)


def _combine_kernel(alpha, tm, dest_hbm, y_hbm, x_ref, gate_ref, mod_ref, g_ref, b_ref, o_ref,
                    d_smem, ybuf, d_sem, y_sem):
    step = pl.program_id(0) * pl.num_programs(1) + pl.program_id(1)
    nsteps = pl.num_programs(0) * pl.num_programs(1)
    slot = step % 2
    nxt = 1 - slot

    def d_copy(blk, sl):
        return pltpu.make_async_copy(dest_hbm.at[blk], d_smem.at[sl], d_sem.at[sl])

    def issue_rows(sl):
        def body(r, carry):
            for kk in range(TOP_K):
                dst = d_smem[sl, kk * tm + r]
                pltpu.make_async_copy(y_hbm.at[pl.ds(dst, 1), :], ybuf.at[sl, kk, pl.ds(r, 1), :],
                                      y_sem.at[sl]).start()
            return carry
        lax.fori_loop(0, tm, body, 0, unroll=4)

    @pl.when(step == 0)
    def _():
        d_copy(0, 0).start()
        d_copy(0, 0).wait()
        issue_rows(0)

        @pl.when(nsteps > 1)
        def _():
            d_copy(1, 1).start()

    @pl.when(step + 1 < nsteps)
    def _():
        d_copy(step + 1, nxt).wait()
        issue_rows(nxt)

    @pl.when(step + 2 < nsteps)
    def _():
        d_copy(step + 2, slot).start()

    for kk in range(TOP_K):
        pltpu.make_async_copy(y_hbm.at[pl.ds(0, tm), :], ybuf.at[slot, kk], y_sem.at[slot]).wait()
    g = gate_ref[...]
    y = ybuf[slot, 0] * g[:, 0:1]
    for kk in range(1, TOP_K):
        y = y + ybuf[slot, kk] * g[:, kk:kk + 1]
    gate_f = mod_ref[5:6, :]
    z = alpha * x_ref[...] + (1.0 + gate_f) * y
    o_ref[...] = _layer_norm(z, g_ref[...], b_ref[...])


def _moe_layer(x, mods_i, w_router, b_router, w_gate_up, b_gate_up, w_down, b_down, ln_g, ln_b, alpha,
               tr=512, te=256, tc=256):
    bsz, s, d = x.shape
    t = bsz * s
    ne = N_EXPERTS
    nst = s // tr
    h, idx, gates, rank, counts = pl.pallas_call(
        functools.partial(_router_kernel, tr),
        grid=(bsz, nst),
        in_specs=[
            pl.BlockSpec((None, tr, d), lambda b, i: (b, i, 0)),
            pl.BlockSpec((None, 6, d), lambda b, i: (b, 0, 0)),
            pl.BlockSpec((ne, d), lambda b, i: (0, 0)),
            pl.BlockSpec((ne, 1), lambda b, i: (0, 0)),
        ],
        out_specs=[
            pl.BlockSpec((tr, d), lambda b, i: (b * nst + i, 0)),
            pl.BlockSpec((TOP_K, tr), lambda b, i: (0, b * nst + i)),
            pl.BlockSpec((TOP_K, tr), lambda b, i: (0, b * nst + i)),
            pl.BlockSpec((TOP_K, tr), lambda b, i: (0, b * nst + i)),
            pl.BlockSpec((ne, 128), lambda b, i: (0, 0)),
        ],
        out_shape=[
            jax.ShapeDtypeStruct((t, d), F32),
            jax.ShapeDtypeStruct((TOP_K, t), jnp.int32),
            jax.ShapeDtypeStruct((TOP_K, t), F32),
            jax.ShapeDtypeStruct((TOP_K, t), jnp.int32),
            jax.ShapeDtypeStruct((ne, 128), F32),
        ],
        scratch_shapes=[pltpu.VMEM((ne, 128), F32)],
        compiler_params=_cparams(("arbitrary", "arbitrary")),
        name="moe_router",
    )(x, mods_i, w_router.T, b_router.reshape(ne, 1))

    cnt = counts[:, 0].astype(jnp.int32)
    padded = (cnt + te - 1) // te * te
    pend = jnp.cumsum(padded)
    pstart = pend - padded
    n_rows = t * TOP_K + ne * te
    n_blocks = n_rows // te
    dest = pstart[idx] + rank
    tok = jnp.broadcast_to(jnp.arange(t, dtype=jnp.int32)[None, :], (TOP_K, t))
    row_tok = jnp.zeros((n_rows,), jnp.int32).at[dest.reshape(-1)].set(tok.reshape(-1))
    blk_expert = jnp.minimum(
        jnp.searchsorted(pend, jnp.arange(n_blocks, dtype=jnp.int32) * te, side="right"), ne - 1).astype(jnp.int32)
    n_used = (pend[-1] // te).astype(jnp.int32).reshape(1)

    dh2 = w_gate_up.shape[-1]
    y_rows = pl.pallas_call(
        functools.partial(_expert_kernel, te),
        grid_spec=pltpu.PrefetchScalarGridSpec(
            num_scalar_prefetch=2,
            grid=(n_blocks,),
            in_specs=[
                pl.BlockSpec(memory_space=pl.ANY),
                pl.BlockSpec(memory_space=pl.ANY),
                pl.BlockSpec((None, d, dh2), lambda j, be, nu: (be[j], 0, 0)),
                pl.BlockSpec((None, 1, dh2), lambda j, be, nu: (be[j], 0, 0)),
                pl.BlockSpec((None, dh2 // 2, d), lambda j, be, nu: (be[j], 0, 0)),
                pl.BlockSpec((None, 1, d), lambda j, be, nu: (be[j], 0, 0)),
            ],
            out_specs=pl.BlockSpec((te, d), lambda j, be, nu: (j, 0)),
            scratch_shapes=[
                pltpu.SMEM((2, te), jnp.int32),
                pltpu.VMEM((2, te, d), F32),
                pltpu.VMEM((d, dh2), BF16),
                pltpu.VMEM((dh2 // 2, d), BF16),
                pltpu.SemaphoreType.DMA((2,)),
                pltpu.SemaphoreType.DMA((2,)),
            ],
        ),
        out_shape=jax.ShapeDtypeStruct((n_rows, d), F32),
        compiler_params=_cparams(("arbitrary",)),
        name="moe_experts",
    )(blk_expert, n_used, row_tok.reshape(n_blocks, te), h, w_gate_up, b_gate_up.reshape(ne, 1, dh2),
      w_down, b_down.reshape(ne, 1, d))

    nct = s // tc
    dest_t = dest.reshape(TOP_K, t // tc, tc).transpose(1, 0, 2).reshape(t // tc, TOP_K * tc)
    gates_t = gates.T
    out = pl.pallas_call(
        functools.partial(_combine_kernel, alpha, tc),
        grid=(bsz, nct),
        in_specs=[
            pl.BlockSpec(memory_space=pl.ANY),
            pl.BlockSpec(memory_space=pl.ANY),
            pl.BlockSpec((None, tc, d), lambda b, i: (b, i, 0)),
            pl.BlockSpec((tc, TOP_K), lambda b, i: (b * nct + i, 0)),
            pl.BlockSpec((None, 6, d), lambda b, i: (b, 0, 0)),
            pl.BlockSpec((1, d), lambda b, i: (0, 0)),
            pl.BlockSpec((1, d), lambda b, i: (0, 0)),
        ],
        out_specs=pl.BlockSpec((None, tc, d), lambda b, i: (b, i, 0)),
        out_shape=jax.ShapeDtypeStruct((bsz, s, d), F32),
        scratch_shapes=[
            pltpu.SMEM((2, TOP_K * tc), jnp.int32),
            pltpu.VMEM((2, TOP_K, tc, d), F32),
            pltpu.SemaphoreType.DMA((2,)),
            pltpu.SemaphoreType.DMA((2,)),
        ],
        compiler_params=_cparams(("arbitrary", "arbitrary")),
        name="moe_combine",
    )(dest_t, y_rows, x, gates_t, mods_i, ln_g.reshape(1, d), ln_b.reshape(1, d))
    return out


def kernel(x, c, ada_w, ada_b, ln_g, ln_b, mla_w_dqkv, mla_g_q, mla_g_kv, mla_w_uq, mla_w_ukv, mla_w_o,
           pool_w, pool_scale, gdn_w_in, gdn_w_conv, gdn_a_log, gdn_dt_bias, gdn_g_norm, gdn_w_o,
           moe_w_router, moe_b_router, moe_w_gate_up, moe_b_gate_up, moe_w_down, moe_b_down):
    depth = ada_w.shape[0]
    alpha = (2.0 * depth) ** 0.25
    mods = _ada_mods(c, ada_w, ada_b)
    for i in range(depth):
        kind, j = i % 3, i // 3
        m_i = mods[i]
        if kind == 0:
            x = _mla_layer(x, m_i, mla_w_dqkv[j], mla_g_q[j], mla_g_kv[j], mla_w_uq[j], mla_w_ukv[j], mla_w_o[j],
                           ln_g[i, 0], ln_b[i, 0], alpha)
        elif kind == 1:
            x = _pool_layer(x, m_i, pool_w[j], pool_scale[j], ln_g[i, 0], ln_b[i, 0], alpha)
        else:
            x = _gdn_layer(x, m_i, gdn_w_in[j], gdn_w_conv[j], gdn_a_log[j], gdn_dt_bias[j], gdn_g_norm[j],
                           gdn_w_o[j], ln_g[i, 0], ln_b[i, 0], alpha)
        x = _moe_layer(x, m_i, moe_w_router[i], moe_b_router[i], moe_w_gate_up[i], moe_b_gate_up[i],
                       moe_w_down[i], moe_b_down[i], ln_g[i, 1], ln_b[i, 1], alpha)
    return x
```

```python
import functools
import math

import jax
import jax.numpy as jnp
from jax import lax
from jax.experimental import pallas as pl
from jax.experimental.pallas import tpu as pltpu

F32 = jnp.float32
BF16 = jnp.bfloat16

CHUNK = 64
LN_EPS = 1e-5
RMS_EPS = 1e-6

MLA_HEADS = 8
MLA_NOPE = 128
MLA_ROPE = 64
MLA_V = 128
MLA_Q_LORA = 384
MLA_KV_LORA = 256
ROPE_THETA = 10000.0
MLA_QK = 256

POOL_WINDOWS = (2, 4, 8, 16)
POOL_HALO = 16

GDN_QK_HEADS = 8
GDN_V_HEADS = 16
GDN_DK = 128
GDN_DV = 128
GDN_CONV = 4
GDN_Q_DIM = GDN_QK_HEADS * GDN_DK
GDN_V_DIM = GDN_V_HEADS * GDN_DV
GDN_CONV_DIM = 2 * GDN_Q_DIM + GDN_V_DIM
GDN_HALO = 8

N_EXPERTS = 32
TOP_K = 4
SWIGLU_LIMIT = 7.0
SWIGLU_ALPHA = 1.702

VMEM_LIMIT = 52 * 1024 * 1024
ROW_TILE = 8


def _cparams(sem):
    return pltpu.CompilerParams(dimension_semantics=sem, vmem_limit_bytes=VMEM_LIMIT)


def _bdot(a, b):
    return jnp.dot(a.astype(BF16), b.astype(BF16), preferred_element_type=F32)


def _bdot_nt(a, b):
    return lax.dot_general(a.astype(BF16), b.astype(BF16), (((1,), (1,)), ((), ())),
                           preferred_element_type=F32)


def _bdot_tn(a, b):
    return lax.dot_general(a.astype(BF16), b.astype(BF16), (((0,), (0,)), ((), ())),
                           preferred_element_type=F32)


def _split(a):
    hi = a.astype(BF16)
    lo = (a - hi.astype(F32)).astype(BF16)
    return hi, lo


def _dot3(a, b):
    ah, al = _split(a)
    bh, bl = _split(b)
    d = functools.partial(jnp.dot, preferred_element_type=F32)
    return d(ah, bh) + (d(ah, bl) + d(al, bh))


def _dot3_nt(a, b):
    ah, al = _split(a)
    bh, bl = _split(b)
    d = lambda x, y: lax.dot_general(x, y, (((1,), (1,)), ((), ())), preferred_element_type=F32)
    return d(ah, bh) + (d(ah, bl) + d(al, bh))


def _layer_norm(z, g, b):
    mu = jnp.mean(z, axis=-1, keepdims=True)
    zc = z - mu
    var = jnp.mean(zc * zc, axis=-1, keepdims=True)
    return zc * lax.rsqrt(var + LN_EPS) * g + b


def _sigmoid(x):
    return 1.0 / (1.0 + jnp.exp(-x))


def _silu(x):
    return x * _sigmoid(x)


def _ada_kernel(c_ref, w_ref, b_ref, o_ref):
    sc = _silu(c_ref[...])
    o_ref[...] = _dot3(sc, w_ref[...]) + b_ref[...]


def _ada_mods(c, ada_w, ada_b):
    depth, d, n = ada_w.shape
    bsz = c.shape[0]
    rows = 8
    tn = 2048
    c8 = jnp.zeros((rows, d), F32).at[:bsz].set(c)
    out = pl.pallas_call(
        _ada_kernel,
        grid=(depth, n // tn),
        in_specs=[
            pl.BlockSpec((rows, d), lambda l, j: (0, 0)),
            pl.BlockSpec((None, d, tn), lambda l, j: (l, 0, j)),
            pl.BlockSpec((None, 1, tn), lambda l, j: (l, 0, j)),
        ],
        out_specs=pl.BlockSpec((None, rows, tn), lambda l, j: (l, 0, j)),
        out_shape=jax.ShapeDtypeStruct((depth, rows, n), F32),
        compiler_params=_cparams(("arbitrary", "arbitrary")),
        name="ada_mods",
    )(c8, ada_w, ada_b.reshape(depth, 1, n))
    return out[:, :bsz].reshape(depth, bsz, 6, d)


def _post_kernel(alpha, gate_row, y_ref, w_ref, x_ref, mod_ref, g_ref, b_ref, o_ref):
    y = jnp.dot(y_ref[...], w_ref[...], preferred_element_type=F32)
    gate = mod_ref[gate_row:gate_row + 1, :]
    z = alpha * x_ref[...] + (1.0 + gate) * y
    o_ref[...] = _layer_norm(z, g_ref[...], b_ref[...])


def _post(y_pre, w_bf, x, mods_i, ln_g, ln_b, alpha, gate_row, tm=512):
    bsz, s, d = x.shape
    kd = y_pre.shape[-1]
    return pl.pallas_call(
        functools.partial(_post_kernel, alpha, gate_row),
        grid=(bsz, s // tm),
        in_specs=[
            pl.BlockSpec((None, tm, kd), lambda b, i: (b, i, 0)),
            pl.BlockSpec((kd, d), lambda b, i: (0, 0)),
            pl.BlockSpec((None, tm, d), lambda b, i: (b, i, 0)),
            pl.BlockSpec((None, 6, d), lambda b, i: (b, 0, 0)),
            pl.BlockSpec((1, d), lambda b, i: (0, 0)),
            pl.BlockSpec((1, d), lambda b, i: (0, 0)),
        ],
        out_specs=pl.BlockSpec((None, tm, d), lambda b, i: (b, i, 0)),
        out_shape=jax.ShapeDtypeStruct((bsz, s, d), F32),
        compiler_params=_cparams(("arbitrary", "arbitrary")),
        name="post_proj_ln",
    )(y_pre, w_bf, x, mods_i, ln_g.reshape(1, d), ln_b.reshape(1, d))


def _mla_proj_kernel(qscale, x_ref, mod_ref, wd_ref, gq_ref, gkv_ref, wuq_ref, wukv_ref, cos_ref, sin_ref,
                     q_ref, k_ref, v_ref):
    shift = mod_ref[0:1, :]
    scale = mod_ref[1:2, :]
    h = x_ref[...] * (1.0 + scale) + shift
    down = _bdot(h, wd_ref[...])
    cq = down[:, :MLA_Q_LORA]
    ckv = down[:, MLA_Q_LORA:MLA_Q_LORA + MLA_KV_LORA]
    kra = down[:, 640:768]
    krb = down[:, 768:896]
    cos = cos_ref[...]
    sin = sin_ref[...]
    k_rope = (kra * cos + krb * sin).astype(BF16)

    cqn = cq * lax.rsqrt(jnp.mean(cq * cq, axis=-1, keepdims=True) + RMS_EPS) * gq_ref[...]
    ckvn = ckv * lax.rsqrt(jnp.mean(ckv * ckv, axis=-1, keepdims=True) + RMS_EPS) * gkv_ref[...]
    cqn = cqn.astype(BF16)
    ckvn = ckvn.astype(BF16)
    nh = MLA_HEADS
    for hh in range(nh):
        qn = jnp.dot(cqn, wuq_ref[:, hh * 128:(hh + 1) * 128], preferred_element_type=F32)
        qa = jnp.dot(cqn, wuq_ref[:, (nh + hh) * 128:(nh + hh + 1) * 128], preferred_element_type=F32)
        qb = jnp.dot(cqn, wuq_ref[:, (2 * nh + hh) * 128:(2 * nh + hh + 1) * 128], preferred_element_type=F32)
        q_ref[hh, :, 0:128] = (qn * qscale).astype(BF16)
        q_ref[hh, :, 128:256] = ((qa * cos + qb * sin) * qscale).astype(BF16)
        kv = jnp.dot(ckvn, wukv_ref[:, hh * 256:(hh + 1) * 256], preferred_element_type=F32)
        k_ref[hh, :, 0:128] = kv[:, :128].astype(BF16)
        k_ref[hh, :, 128:256] = k_rope
        v_ref[hh, :, :] = kv[:, 128:].astype(BF16)


def _mla_attn_kernel(tq, tk, q_ref, k_ref, v_ref, o_ref):
    i = pl.program_id(2)
    q = q_ref[...]

    def block(j, carry, masked):
        m, l, acc = carry
        start = pl.multiple_of(j * tk, tk)
        kb = k_ref[pl.ds(start, tk), :]
        vb = v_ref[pl.ds(start, tk), :]
        s = lax.dot_general(q, kb, (((1,), (1,)), ((), ())), preferred_element_type=F32)
        if masked:
            qc = (i * tq + lax.broadcasted_iota(jnp.int32, (tq, tk), 0)) // CHUNK
            kc = (j * tk + lax.broadcasted_iota(jnp.int32, (tq, tk), 1)) // CHUNK
            s = jnp.where(kc <= qc, s, -1e30)
        m_new = jnp.maximum(m, jnp.max(s, axis=-1, keepdims=True))
        p = jnp.exp(s - m_new)
        corr = jnp.exp(m - m_new)
        l_new = corr * l + jnp.sum(p, axis=-1, keepdims=True)
        acc_new = corr * acc + jnp.dot(p.astype(BF16), vb, preferred_element_type=F32)
        return m_new, l_new, acc_new

    init = (jnp.full((tq, 1), -1e30, F32), jnp.zeros((tq, 1), F32), jnp.zeros((tq, MLA_V), F32))
    carry = lax.fori_loop(0, i, lambda j, c: block(j, c, False), init)
    m, l, acc = block(i, carry, True)
    o_ref[...] = (acc / l).astype(BF16)


def _mla_layer(x, mods_i, w_dqkv, g_q, g_kv, w_uq, w_ukv, w_o, ln_g, ln_b, alpha, tm=512, tq=512):
    bsz, s, d = x.shape
    nh = MLA_HEADS
    half = MLA_ROPE // 2
    k1 = w_dqkv[:, 640:640 + half]
    k2 = w_dqkv[:, 640 + half:704]
    z64 = jnp.zeros((d, 64), F32)
    wd = jnp.concatenate([w_dqkv[:, :640], k1, k2, z64, k2, k1, z64], axis=1).astype(BF16)
    wq = w_uq.reshape(MLA_Q_LORA, nh, MLA_NOPE + MLA_ROPE)
    wq_n = wq[:, :, :MLA_NOPE].reshape(MLA_Q_LORA, nh * 128)
    x1 = wq[:, :, MLA_NOPE:MLA_NOPE + half]
    x2 = wq[:, :, MLA_NOPE + half:]
    zq = jnp.zeros((MLA_Q_LORA, nh, 64), F32)
    wq_a = jnp.concatenate([x1, x2, zq], axis=2).reshape(MLA_Q_LORA, nh * 128)
    wq_b = jnp.concatenate([x2, x1, zq], axis=2).reshape(MLA_Q_LORA, nh * 128)
    wuq = jnp.concatenate([wq_n, wq_a, wq_b], axis=1).astype(BF16)
    wukv = w_ukv.astype(BF16)
    pos = jnp.arange(s, dtype=F32)
    inv_freq = ROPE_THETA ** (-jnp.arange(0, MLA_ROPE, 2, dtype=F32) / MLA_ROPE)
    ang = pos[:, None] * inv_freq[None, :]
    cs, sn = jnp.cos(ang), jnp.sin(ang)
    zt = jnp.zeros((s, 64), F32)
    cos_t = jnp.concatenate([cs, cs, zt], axis=1)
    sin_t = jnp.concatenate([-sn, sn, zt], axis=1)
    qscale = (MLA_NOPE + MLA_ROPE) ** -0.5

    q, k, v = pl.pallas_call(
        functools.partial(_mla_proj_kernel, qscale),
        grid=(bsz, s // tm),
        in_specs=[
            pl.BlockSpec((None, tm, d), lambda b, i: (b, i, 0)),
            pl.BlockSpec((None, 6, d), lambda b, i: (b, 0, 0)),
            pl.BlockSpec(wd.shape, lambda b, i: (0, 0)),
            pl.BlockSpec((1, MLA_Q_LORA), lambda b, i: (0, 0)),
            pl.BlockSpec((1, MLA_KV_LORA), lambda b, i: (0, 0)),
            pl.BlockSpec(wuq.shape, lambda b, i: (0, 0)),
            pl.BlockSpec(wukv.shape, lambda b, i: (0, 0)),
            pl.BlockSpec((tm, 128), lambda b, i: (i, 0)),
            pl.BlockSpec((tm, 128), lambda b, i: (i, 0)),
        ],
        out_specs=[
            pl.BlockSpec((None, nh, tm, MLA_QK), lambda b, i: (b, 0, i, 0)),
            pl.BlockSpec((None, nh, tm, MLA_QK), lambda b, i: (b, 0, i, 0)),
            pl.BlockSpec((None, nh, tm, MLA_V), lambda b, i: (b, 0, i, 0)),
        ],
        out_shape=[
            jax.ShapeDtypeStruct((bsz, nh, s, MLA_QK), BF16),
            jax.ShapeDtypeStruct((bsz, nh, s, MLA_QK), BF16),
            jax.ShapeDtypeStruct((bsz, nh, s, MLA_V), BF16),
        ],
        compiler_params=_cparams(("arbitrary", "arbitrary")),
        name="mla_proj",
    )(x, mods_i, wd, g_q.reshape(1, -1), g_kv.reshape(1, -1), wuq, wukv, cos_t, sin_t)

    o = pl.pallas_call(
        functools.partial(_mla_attn_kernel, tq, tq),
        grid=(bsz, nh, s // tq),
        in_specs=[
            pl.BlockSpec((None, None, tq, MLA_QK), lambda b, h, i: (b, h, i, 0)),
            pl.BlockSpec((None, None, s, MLA_QK), lambda b, h, i: (b, h, 0, 0)),
            pl.BlockSpec((None, None, s, MLA_V), lambda b, h, i: (b, h, 0, 0)),
        ],
        out_specs=pl.BlockSpec((None, tq, MLA_V), lambda b, h, i: (b, i, h)),
        out_shape=jax.ShapeDtypeStruct((bsz, s, nh * MLA_V), BF16),
        compiler_params=_cparams(("arbitrary", "arbitrary", "arbitrary")),
        name="mla_attn",
    )(q, k, v)

    return _post(o, w_o.astype(BF16), x, mods_i, ln_g, ln_b, alpha, gate_row=2)


def _pool_kernel(alpha, tm, x_ref, halo_ref, mod_ref, w_ref, cs_ref, g_ref, b_ref, o_ref, buf):
    i = pl.program_id(1)
    d = x_ref.shape[-1]
    gd = d // len(POOL_WINDOWS)
    shift = mod_ref[0:1, :]
    scale = mod_ref[1:2, :]
    gate = mod_ref[2:3, :]
    x = x_ref[...]
    h = x * (1.0 + scale) + shift
    hh = halo_ref[...] * (1.0 + scale) + shift
    hh = jnp.where(i > 0, hh, 0.0)
    buf[0:POOL_HALO, :] = hh
    buf[POOL_HALO:, :] = h
    t = i * tm + lax.broadcasted_iota(jnp.int32, (tm, 1), 0)
    ys = []
    for g, win in enumerate(POOL_WINDOWS):
        cols = slice(g * gd, (g + 1) * gd)
        acc = buf[POOL_HALO:POOL_HALO + tm, cols]
        for j in range(1, win):
            acc = acc + buf[POOL_HALO - j:POOL_HALO - j + tm, cols]
        count = jnp.minimum(t + 1, win).astype(F32)
        dg = acc / count - h[:, cols]
        ys.append(_bdot(dg, w_ref[g]))
    y = jnp.concatenate(ys, axis=-1) * cs_ref[...]
    z = alpha * x + (1.0 + gate) * y
    o_ref[...] = _layer_norm(z, g_ref[...], b_ref[...])


def _pool_layer(x, mods_i, w_pool, ch_scale, ln_g, ln_b, alpha, tm=512):
    bsz, s, d = x.shape
    gd = d // len(POOL_WINDOWS)
    hb = tm // POOL_HALO
    return pl.pallas_call(
        functools.partial(_pool_kernel, alpha, tm),
        grid=(bsz, s // tm),
        in_specs=[
            pl.BlockSpec((None, tm, d), lambda b, i: (b, i, 0)),
            pl.BlockSpec((None, POOL_HALO, d), lambda b, i: (b, jnp.maximum(i * hb - 1, 0), 0)),
            pl.BlockSpec((None, 6, d), lambda b, i: (b, 0, 0)),
            pl.BlockSpec((len(POOL_WINDOWS), gd, gd), lambda b, i: (0, 0, 0)),
            pl.BlockSpec((1, d), lambda b, i: (0, 0)),
            pl.BlockSpec((1, d), lambda b, i: (0, 0)),
            pl.BlockSpec((1, d), lambda b, i: (0, 0)),
        ],
        out_specs=pl.BlockSpec((None, tm, d), lambda b, i: (b, i, 0)),
        out_shape=jax.ShapeDtypeStruct((bsz, s, d), F32),
        scratch_shapes=[pltpu.VMEM((tm + POOL_HALO, d), F32)],
        compiler_params=_cparams(("arbitrary", "arbitrary")),
        name="pool_layer",
    )(x, x, mods_i, w_pool.astype(BF16), ch_scale.reshape(1, d), ln_g.reshape(1, d), ln_b.reshape(1, d))


def _gdn_proj_kernel(tm, x_ref, halo_ref, mod_ref, wqkv_ref, wz_ref, wba_ref, wc_ref, alog_ref, dtb_ref,
                     q_ref, k_ref, v_ref, z_ref, beta_ref, gc_ref, hbuf, pbuf):
    i = pl.program_id(1)
    shift = mod_ref[0:1, :]
    scale = mod_ref[1:2, :]
    h = x_ref[...] * (1.0 + scale) + shift
    hh = halo_ref[...] * (1.0 + scale) + shift
    hh = jnp.where(i > 0, hh, 0.0)
    hbuf[0:GDN_HALO, :] = hh
    hbuf[GDN_HALO:, :] = h
    h_ext = hbuf[...].astype(BF16)

    z_ref[...] = jnp.dot(h.astype(BF16), wz_ref[...], preferred_element_type=F32).astype(BF16)

    ba = _dot3(h, wba_ref[...])
    nv = GDN_V_HEADS
    beta_ref[...] = _sigmoid(ba[:, :nv])
    a = ba[:, nv:] + dtb_ref[...]
    softplus = jnp.maximum(a, 0.0) + jnp.log(1.0 + jnp.exp(-jnp.abs(a)))
    g = -jnp.exp(alog_ref[...]) * softplus
    r = lax.broadcasted_iota(jnp.int32, (tm, tm), 0)
    c = lax.broadcasted_iota(jnp.int32, (tm, tm), 1)
    tri = jnp.where((c <= r) & (c // CHUNK == r // CHUNK), 1.0, 0.0).astype(BF16)
    g_hi = g.astype(BF16)
    g_lo = (g - g_hi.astype(F32)).astype(BF16)
    g_lo2 = (g - g_hi.astype(F32) - g_lo.astype(F32)).astype(BF16)
    d = functools.partial(jnp.dot, preferred_element_type=F32)
    gc_ref[...] = d(tri, g_hi) + (d(tri, g_lo) + d(tri, g_lo2))

    cw = 512
    for cb in range(GDN_CONV_DIM // cw):
        cols = slice(cb * cw, (cb + 1) * cw)
        pbuf[...] = jnp.dot(h_ext, wqkv_ref[:, cols], preferred_element_type=F32)
        acc = pbuf[GDN_HALO:GDN_HALO + tm, :] * wc_ref[GDN_CONV - 1:GDN_CONV, cols]
        for j in range(GDN_CONV - 1):
            off = GDN_HALO - (GDN_CONV - 1) + j
            acc = acc + pbuf[off:off + tm, :] * wc_ref[j:j + 1, cols]
        y = _silu(acc)
        if cb * cw < 2 * GDN_Q_DIM:
            outs = []
            for hd in range(cw // GDN_DK):
                yh = y[:, hd * GDN_DK:(hd + 1) * GDN_DK]
                outs.append(yh * lax.rsqrt(jnp.sum(yh * yh, axis=-1, keepdims=True) + RMS_EPS))
            y = jnp.concatenate(outs, axis=-1)
            if cb * cw < GDN_Q_DIM:
                q_ref[:, cols] = y * (GDN_DK ** -0.5)
            else:
                k_ref[:, cb * cw - GDN_Q_DIM:(cb + 1) * cw - GDN_Q_DIM] = y
        else:
            v_ref[:, cb * cw - 2 * GDN_Q_DIM:(cb + 1) * cw - 2 * GDN_Q_DIM] = y


def _col(x, hidx):
    lane = lax.broadcasted_iota(jnp.int32, x.shape, 1)
    return jnp.sum(jnp.where(lane == hidx, x, 0.0), axis=-1, keepdims=True)


GDN_HPS = 4


def _tile_rows(x, n):
    return jnp.concatenate([x] * n, axis=0)


def _gdn_chunk_kernel(tb, q_ref, k_ref, v_ref, z_ref, beta_ref, gc_ref, gr_ref, gn_ref, o_ref, s01, s23):
    hg = pl.program_id(1)
    n = pl.program_id(2)
    cs = CHUNK
    hp = GDN_HPS
    dv = GDN_DV
    wp = hp * cs
    wv = hp * dv

    @pl.when(n == 0)
    def _():
        s01[...] = jnp.zeros_like(s01)
        s23[...] = jnp.zeros_like(s23)

    bcols = [_col(beta_ref[...], hp * hg + i) for i in range(hp)]
    gcols = [_col(gc_ref[...], hp * hg + i) for i in range(hp)]
    beta_b = jnp.concatenate([jnp.broadcast_to(cc, (tb, dv)) for cc in bcols], axis=1)
    g_b = jnp.concatenate([jnp.broadcast_to(cc, (tb, dv)) for cc in gcols], axis=1)
    lane_p = lax.broadcasted_iota(jnp.int32, (tb, wp), 1)
    g_p = jnp.broadcast_to(gcols[hp - 1], (tb, wp))
    for i in reversed(range(hp - 1)):
        g_p = jnp.where(lane_p < (i + 1) * cs, gcols[i], g_p)

    r = lax.broadcasted_iota(jnp.int32, (cs, wp), 0)
    lc = lax.broadcasted_iota(jnp.int32, (cs, wp), 1) & (cs - 1)
    tril_incl = r >= lc
    tril_strict = r > lc
    eye_p = jnp.where(r == lc, 1.0, 0.0).astype(F32)
    rr = lax.broadcasted_iota(jnp.int32, (wp, wp), 0)
    ll = lax.broadcasted_iota(jnp.int32, (wp, wp), 1)
    bd_mask = (rr // cs) == (ll // cs)
    pair_mask = (rr // dv) == (ll // dv)
    rv = lax.broadcasted_iota(jnp.int32, (wp, wv), 0)
    lv = lax.broadcasted_iota(jnp.int32, (wp, wv), 1)
    bdv_mask = (rv // cs) == (lv // dv)

    def bd(y):
        return jnp.where(bd_mask, _tile_rows(y, hp), 0.0).astype(BF16)

    def bdv(y):
        return jnp.where(bdv_mask, _tile_rows(y, hp), 0.0).astype(BF16)

    def mm(a, b):
        return jnp.dot(a.astype(BF16), b, preferred_element_type=F32)

    def mm_nt(a, b):
        return lax.dot_general(a.astype(BF16), b, (((1,), (1,)), ((), ())), preferred_element_type=F32)

    states = [s01[...], s23[...]]
    gn = jnp.concatenate([gn_ref[...]] * hp, axis=1)
    for c in range(tb // cs):
        rows = slice(c * cs, (c + 1) * cs)
        q2 = q_ref[rows, :]
        k2 = k_ref[rows, :]
        q4 = jnp.concatenate([q2[:, :dv], q2[:, :dv], q2[:, dv:], q2[:, dv:]], axis=1)
        k4 = jnp.concatenate([k2[:, :dv], k2[:, :dv], k2[:, dv:], k2[:, dv:]], axis=1)
        v4 = v_ref[rows, :]
        bb = beta_b[rows, :]
        gb = g_b[rows, :]
        gp = g_p[rows, :]
        grow = gr_ref[c:c + 1, :]
        decay = jnp.exp(jnp.where(tril_incl, gp - grow, -jnp.inf))
        eg = jnp.exp(gb)
        glast = gb[cs - 1:cs, :]
        kb = k4 * bb
        vb = v4 * bb
        kbd = bdv(k4)
        kk = mm_nt(kb, kbd)
        xm = -jnp.where(tril_strict, kk * decay, 0.0)
        tinv = eye_p + xm
        pw = xm
        bdp = bd(pw)
        for _ in range(5):
            pw = mm(pw, bdp)
            bdp = bd(pw)
            tinv = tinv + mm(tinv, bdp)
        u = mm(tinv, bdv(vb))
        w = mm(tinv, bdv(kb * eg))
        a_intra = mm_nt(q4, kbd) * decay
        qg = q4 * eg
        kd = k4 * jnp.exp(glast - gb)
        gl = jnp.exp(glast)
        vn, oq = [], []
        for p in range(2):
            sl = slice(p * 2 * dv, (p + 1) * 2 * dv)
            sb = states[p].astype(BF16)
            vn.append(u[:, sl] - mm(w[:, sl], sb))
            oq.append(mm(qg[:, sl], sb))
        v_new = jnp.concatenate(vn, axis=1)
        o = jnp.concatenate(oq, axis=1) + mm(a_intra, bdv(v_new))
        for p in range(2):
            sl = slice(p * 2 * dv, (p + 1) * 2 * dv)
            upd = lax.dot_general(kd[:, sl].astype(BF16), v_new[:, sl].astype(BF16), (((0,), (0,)), ((), ())),
                                  preferred_element_type=F32)
            states[p] = states[p] * gl[:, sl] + jnp.where(pair_mask, upd, 0.0)
        outs = []
        for i in range(hp):
            oi = o[:, i * dv:(i + 1) * dv]
            outs.append(oi * lax.rsqrt(jnp.mean(oi * oi, axis=-1, keepdims=True) + RMS_EPS))
        on = jnp.concatenate(outs, axis=1) * gn
        zz = z_ref[rows, :].astype(F32)
        o_ref[rows, :] = (on * _silu(zz)).astype(BF16)
    s01[...] = states[0]
    s23[...] = states[1]


def _gdn_layer(x, mods_i, w_in, w_conv, a_log, dt_bias, g_norm, w_o, ln_g, ln_b, alpha, tm=256, tb=512):
    bsz, s, d = x.shape
    nv = GDN_V_HEADS
    wqkv = w_in[:, :GDN_CONV_DIM].astype(BF16)
    wz = w_in[:, GDN_CONV_DIM:GDN_CONV_DIM + GDN_V_DIM].astype(BF16)
    wba = w_in[:, GDN_CONV_DIM + GDN_V_DIM:]
    hb = tm // GDN_HALO
    q, k, v, z, beta, gc = pl.pallas_call(
        functools.partial(_gdn_proj_kernel, tm),
        grid=(bsz, s // tm),
        in_specs=[
            pl.BlockSpec((None, tm, d), lambda b, i: (b, i, 0)),
            pl.BlockSpec((None, GDN_HALO, d), lambda b, i: (b, jnp.maximum(i * hb - 1, 0), 0)),
            pl.BlockSpec((None, 6, d), lambda b, i: (b, 0, 0)),
            pl.BlockSpec(wqkv.shape, lambda b, i: (0, 0)),
            pl.BlockSpec(wz.shape, lambda b, i: (0, 0)),
            pl.BlockSpec(wba.shape, lambda b, i: (0, 0)),
            pl.BlockSpec((GDN_CONV, GDN_CONV_DIM), lambda b, i: (0, 0)),
            pl.BlockSpec((1, nv), lambda b, i: (0, 0)),
            pl.BlockSpec((1, nv), lambda b, i: (0, 0)),
        ],
        out_specs=[
            pl.BlockSpec((None, tm, GDN_Q_DIM), lambda b, i: (b, i, 0)),
            pl.BlockSpec((None, tm, GDN_Q_DIM), lambda b, i: (b, i, 0)),
            pl.BlockSpec((None, tm, GDN_V_DIM), lambda b, i: (b, i, 0)),
            pl.BlockSpec((None, tm, GDN_V_DIM), lambda b, i: (b, i, 0)),
            pl.BlockSpec((None, tm, nv), lambda b, i: (b, i, 0)),
            pl.BlockSpec((None, tm, nv), lambda b, i: (b, i, 0)),
        ],
        out_shape=[
            jax.ShapeDtypeStruct((bsz, s, GDN_Q_DIM), F32),
            jax.ShapeDtypeStruct((bsz, s, GDN_Q_DIM), F32),
            jax.ShapeDtypeStruct((bsz, s, GDN_V_DIM), F32),
            jax.ShapeDtypeStruct((bsz, s, GDN_V_DIM), BF16),
            jax.ShapeDtypeStruct((bsz, s, nv), F32),
            jax.ShapeDtypeStruct((bsz, s, nv), F32),
        ],
        scratch_shapes=[pltpu.VMEM((tm + GDN_HALO, d), F32), pltpu.VMEM((tm + GDN_HALO, 512), F32)],
        compiler_params=_cparams(("arbitrary", "arbitrary")),
        name="gdn_proj",
    )(x, x, mods_i, wqkv, wz, wba, w_conv, a_log.reshape(1, nv), dt_bias.reshape(1, nv))

    hp = GDN_HPS
    nc = s // CHUNK
    gr = gc.reshape(bsz, nc, CHUNK, nv // hp, hp).transpose(0, 3, 1, 4, 2).reshape(bsz, nv // hp, nc, hp * CHUNK)
    rep = GDN_V_HEADS // GDN_QK_HEADS
    cpb = tb // CHUNK
    qw = hp // rep * GDN_DK
    o = pl.pallas_call(
        functools.partial(_gdn_chunk_kernel, tb),
        grid=(bsz, nv // hp, s // tb),
        in_specs=[
            pl.BlockSpec((None, tb, qw), lambda b, h, n: (b, n, h)),
            pl.BlockSpec((None, tb, qw), lambda b, h, n: (b, n, h)),
            pl.BlockSpec((None, tb, hp * GDN_DV), lambda b, h, n: (b, n, h)),
            pl.BlockSpec((None, tb, hp * GDN_DV), lambda b, h, n: (b, n, h)),
            pl.BlockSpec((None, tb, nv), lambda b, h, n: (b, n, 0)),
            pl.BlockSpec((None, tb, nv), lambda b, h, n: (b, n, 0)),
            pl.BlockSpec((None, None, cpb, hp * CHUNK), lambda b, h, n: (b, h, n, 0)),
            pl.BlockSpec((1, GDN_DV), lambda b, h, n: (0, 0)),
        ],
        out_specs=pl.BlockSpec((None, tb, hp * GDN_DV), lambda b, h, n: (b, n, h)),
        out_shape=jax.ShapeDtypeStruct((bsz, s, GDN_V_DIM), BF16),
        scratch_shapes=[pltpu.VMEM((2 * GDN_DK, 2 * GDN_DV), F32), pltpu.VMEM((2 * GDN_DK, 2 * GDN_DV), F32)],
        compiler_params=_cparams(("arbitrary", "arbitrary", "arbitrary")),
        name="gdn_chunk",
    )(q, k, v, z, beta, gc, gr, g_norm.reshape(1, GDN_DV))

    return _post(o, w_o.astype(BF16), x, mods_i, ln_g, ln_b, alpha, gate_row=2)


def _router_kernel(tm, x_ref, mod_ref, wr_ref, br_ref, h_ref, idx_ref, gate_ref, rank_ref, cnt_ref, run):
    step = pl.program_id(0) * pl.num_programs(1) + pl.program_id(1)

    @pl.when(step == 0)
    def _():
        run[...] = jnp.zeros_like(run)

    shift = mod_ref[3:4, :]
    scale = mod_ref[4:5, :]
    h = x_ref[...] * (1.0 + scale) + shift
    nsc = h.shape[-1] // 128
    for sc in range(nsc):
        h_ref[pl.ds(sc, tm, stride=nsc), :] = h[:, sc * 128:(sc + 1) * 128]
    logits = _dot3_nt(wr_ref[...], h) + br_ref[...]
    eid = lax.broadcasted_iota(jnp.int32, (N_EXPERTS, tm), 0)
    vals, idxs, hots = [], [], []
    cur = logits
    for _ in range(TOP_K):
        m = jnp.max(cur, axis=0, keepdims=True)
        sel = jnp.min(jnp.where(cur == m, eid, N_EXPERTS), axis=0, keepdims=True)
        hot = eid == sel
        vals.append(m)
        idxs.append(sel)
        hots.append(hot)
        cur = jnp.where(hot, -jnp.inf, cur)
    es = [jnp.exp(v - vals[0]) for v in vals]
    den = es[0] + es[1] + es[2] + es[3]
    cnt = jnp.zeros((N_EXPERTS, tm), F32)
    for hot in hots:
        cnt = cnt + jnp.where(hot, 1.0, 0.0)
    r = lax.broadcasted_iota(jnp.int32, (tm, tm), 0)
    c = lax.broadcasted_iota(jnp.int32, (tm, tm), 1)
    before = jnp.where(r < c, 1.0, 0.0).astype(BF16)
    prefix = jnp.dot(cnt.astype(BF16), before, preferred_element_type=F32) + run[:, 0:1]
    for kk in range(TOP_K):
        idx_ref[kk:kk + 1, :] = idxs[kk]
        gate_ref[kk:kk + 1, :] = es[kk] / den
        rank_ref[kk:kk + 1, :] = jnp.sum(jnp.where(hots[kk], prefix, 0.0), axis=0, keepdims=True).astype(jnp.int32)
    run[...] = run[...] + jnp.sum(cnt, axis=1, keepdims=True)
    cnt_ref[...] = run[...]


def _expert_kernel(tm, be_ref, nu_ref, tok_hbm, h_hbm, wgu_ref, bgu_ref, wd_ref, bd_ref, y_ref,
                   tok_smem, xbuf, wgu_bf, wd_bf, tok_sem, x_sem):
    j = pl.program_id(0)
    n_used = nu_ref[0]
    slot = j % 2
    nxt = 1 - slot

    def tok_copy(blk, sl):
        return pltpu.make_async_copy(tok_hbm.at[blk], tok_smem.at[sl], tok_sem.at[sl])

    def issue_rows(sl):
        def body(r, carry):
            t = tok_smem[sl, r]
            src = h_hbm.at[pl.ds(pl.multiple_of(t * ROW_TILE, ROW_TILE), ROW_TILE), :]
            dst = xbuf.at[sl, pl.ds(pl.multiple_of(r * ROW_TILE, ROW_TILE), ROW_TILE), :]
            pltpu.make_async_copy(src, dst, x_sem.at[sl]).start()
            return carry
        lax.fori_loop(0, tm, body, 0, unroll=8)

    @pl.when(j == 0)
    def _():
        tok_copy(0, 0).start()
        tok_copy(0, 0).wait()
        issue_rows(0)

        @pl.when(n_used > 1)
        def _():
            tok_copy(1, 1).start()

    @pl.when(j + 1 < n_used)
    def _():
        tok_copy(j + 1, nxt).wait()
        issue_rows(nxt)

    @pl.when(j + 2 < n_used)
    def _():
        tok_copy(j + 2, slot).start()

    changed = jnp.logical_or(j == 0, be_ref[j] != be_ref[jnp.maximum(j - 1, 0)])

    @pl.when(jnp.logical_and(changed, j < n_used))
    def _():
        wgu_bf[...] = wgu_ref[...].astype(BF16)
        wd_bf[...] = wd_ref[...].astype(BF16)

    @pl.when(j < n_used)
    def _():
        pltpu.make_async_copy(h_hbm.at[pl.ds(0, tm * ROW_TILE), :], xbuf.at[slot], x_sem.at[slot]).wait()
        nsc = ROW_TILE
        xb = jnp.concatenate([xbuf[slot, pl.ds(sc, tm, stride=nsc), :] for sc in range(nsc)], axis=-1).astype(BF16)
        gu = jnp.dot(xb, wgu_bf[...], preferred_element_type=F32) + bgu_ref[...]
        dh = gu.shape[-1] // 2
        gate = jnp.minimum(gu[:, :dh], SWIGLU_LIMIT)
        up = jnp.clip(gu[:, dh:], -SWIGLU_LIMIT, SWIGLU_LIMIT)
        act = gate * _sigmoid(SWIGLU_ALPHA * gate) * (up + 1.0)
        y = jnp.dot(act.astype(BF16), wd_bf[...], preferred_element_type=F32) + bd_ref[...]
        for sc in range(nsc):
            y_ref[pl.ds(sc, tm, stride=nsc), :] = y[:, sc * 128:(sc + 1) * 128]

    @pl.when(j >= n_used)
    def _():
        y_ref[...] = jnp.zeros_like(y_ref)


def _combine_kernel(alpha, tm, dest_hbm, y_hbm, x_ref, gate_ref, mod_ref, g_ref, b_ref, o_ref,
                    d_smem, ybuf, d_sem, y_sem):
    step = pl.program_id(0) * pl.num_programs(1) + pl.program_id(1)
    nsteps = pl.num_programs(0) * pl.num_programs(1)
    slot = step % 2
    nxt = 1 - slot

    def d_copy(blk, sl):
        return pltpu.make_async_copy(dest_hbm.at[blk], d_smem.at[sl], d_sem.at[sl])

    def issue_rows(sl):
        def body(r, carry):
            for kk in range(TOP_K):
                dst = d_smem[sl, kk * tm + r]
                src = y_hbm.at[pl.ds(pl.multiple_of(dst * ROW_TILE, ROW_TILE), ROW_TILE), :]
                dbuf = ybuf.at[sl, kk, pl.ds(pl.multiple_of(r * ROW_TILE, ROW_TILE), ROW_TILE), :]
                pltpu.make_async_copy(src, dbuf, y_sem.at[sl]).start()
            return carry
        lax.fori_loop(0, tm, body, 0, unroll=4)

    @pl.when(step == 0)
    def _():
        d_copy(0, 0).start()
        d_copy(0, 0).wait()
        issue_rows(0)

        @pl.when(nsteps > 1)
        def _():
            d_copy(1, 1).start()

    @pl.when(step + 1 < nsteps)
    def _():
        d_copy(step + 1, nxt).wait()
        issue_rows(nxt)

    @pl.when(step + 2 < nsteps)
    def _():
        d_copy(step + 2, slot).start()

    for kk in range(TOP_K):
        pltpu.make_async_copy(y_hbm.at[pl.ds(0, tm * ROW_TILE), :], ybuf.at[slot, kk], y_sem.at[slot]).wait()
    g = gate_ref[...]
    parts = []
    for sc in range(ROW_TILE):
        acc = ybuf[slot, 0, pl.ds(sc, tm, stride=ROW_TILE), :] * g[:, 0:1]
        for kk in range(1, TOP_K):
            acc = acc + ybuf[slot, kk, pl.ds(sc, tm, stride=ROW_TILE), :] * g[:, kk:kk + 1]
        parts.append(acc)
    y = jnp.concatenate(parts, axis=-1)
    gate_f = mod_ref[5:6, :]
    z = alpha * x_ref[...] + (1.0 + gate_f) * y
    o_ref[...] = _layer_norm(z, g_ref[...], b_ref[...])


def _moe_layer(x, mods_i, w_router, b_router, w_gate_up, b_gate_up, w_down, b_down, ln_g, ln_b, alpha,
               layer=0, tr=512, te=256, tc=256):
    bsz, s, d = x.shape
    assert d == ROW_TILE * 128, "token rows are stored as one (8,128) f32 tile"
    t = bsz * s
    ne = N_EXPERTS
    nst = s // tr
    h, idx, gates, rank, counts = pl.pallas_call(
        functools.partial(_router_kernel, tr),
        grid=(bsz, nst),
        in_specs=[
            pl.BlockSpec((None, tr, d), lambda b, i: (b, i, 0)),
            pl.BlockSpec((None, 6, d), lambda b, i: (b, 0, 0)),
            pl.BlockSpec((ne, d), lambda b, i: (0, 0)),
            pl.BlockSpec((ne, 1), lambda b, i: (0, 0)),
        ],
        out_specs=[
            pl.BlockSpec((tr * ROW_TILE, 128), lambda b, i: (b * nst + i, 0)),
            pl.BlockSpec((TOP_K, tr), lambda b, i: (0, b * nst + i)),
            pl.BlockSpec((TOP_K, tr), lambda b, i: (0, b * nst + i)),
            pl.BlockSpec((TOP_K, tr), lambda b, i: (0, b * nst + i)),
            pl.BlockSpec((ne, 128), lambda b, i: (0, 0)),
        ],
        out_shape=[
            jax.ShapeDtypeStruct((t * ROW_TILE, 128), F32),
            jax.ShapeDtypeStruct((TOP_K, t), jnp.int32),
            jax.ShapeDtypeStruct((TOP_K, t), F32),
            jax.ShapeDtypeStruct((TOP_K, t), jnp.int32),
            jax.ShapeDtypeStruct((ne, 128), F32),
        ],
        scratch_shapes=[pltpu.VMEM((ne, 128), F32)],
        compiler_params=_cparams(("arbitrary", "arbitrary")),
        name="moe_router",
    )(x, mods_i, w_router.T, b_router.reshape(ne, 1))

    cnt = counts[:, 0].astype(jnp.int32)
    padded = (cnt + te - 1) // te * te
    pend = jnp.cumsum(padded)
    pstart = pend - padded
    n_rows = t * TOP_K + ne * te
    n_blocks = n_rows // te
    eids = jnp.arange(ne, dtype=jnp.int32)
    dest = jnp.sum(jnp.where(idx[None] == eids[:, None, None], pstart[:, None, None], 0), axis=0) + rank
    tok = jnp.broadcast_to(jnp.arange(t, dtype=jnp.int32)[None, :], (TOP_K, t))
    row_tok = jnp.zeros((n_rows,), jnp.int32).at[dest.reshape(-1)].set(tok.reshape(-1))
    blk_start = jnp.arange(n_blocks, dtype=jnp.int32) * te
    blk_expert = jnp.minimum(jnp.sum((blk_start[:, None] >= pend[None, :]).astype(jnp.int32), axis=1), ne - 1)
    blk_expert = blk_expert + layer * ne
    n_used = (pend[-1] // te).astype(jnp.int32).reshape(1)

    dh2 = w_gate_up.shape[-1]
    w_gate_up = w_gate_up.reshape(-1, d, dh2)
    b_gate_up = b_gate_up.reshape(-1, 1, dh2)
    w_down = w_down.reshape(-1, dh2 // 2, d)
    b_down = b_down.reshape(-1, 1, d)
    y_rows = pl.pallas_call(
        functools.partial(_expert_kernel, te),
        grid_spec=pltpu.PrefetchScalarGridSpec(
            num_scalar_prefetch=2,
            grid=(n_blocks,),
            in_specs=[
                pl.BlockSpec(memory_space=pl.ANY),
                pl.BlockSpec(memory_space=pl.ANY),
                pl.BlockSpec((None, d, dh2), lambda j, be, nu: (be[j], 0, 0)),
                pl.BlockSpec((None, 1, dh2), lambda j, be, nu: (be[j], 0, 0)),
                pl.BlockSpec((None, dh2 // 2, d), lambda j, be, nu: (be[j], 0, 0)),
                pl.BlockSpec((None, 1, d), lambda j, be, nu: (be[j], 0, 0)),
            ],
            out_specs=pl.BlockSpec((te * ROW_TILE, 128), lambda j, be, nu: (j, 0)),
            scratch_shapes=[
                pltpu.SMEM((2, te), jnp.int32),
                pltpu.VMEM((2, te * ROW_TILE, 128), F32),
                pltpu.VMEM((d, dh2), BF16),
                pltpu.VMEM((dh2 // 2, d), BF16),
                pltpu.SemaphoreType.DMA((2,)),
                pltpu.SemaphoreType.DMA((2,)),
            ],
        ),
        out_shape=jax.ShapeDtypeStruct((n_rows * ROW_TILE, 128), F32),
        compiler_params=_cparams(("arbitrary",)),
        name="moe_experts",
    )(blk_expert, n_used, row_tok.reshape(n_blocks, te), h, w_gate_up, b_gate_up, w_down, b_down)

    nct = s // tc
    dest_t = dest.reshape(TOP_K, t // tc, tc).transpose(1, 0, 2).reshape(t // tc, TOP_K * tc)
    gates_t = gates.T
    out = pl.pallas_call(
        functools.partial(_combine_kernel, alpha, tc),
        grid=(bsz, nct),
        in_specs=[
            pl.BlockSpec(memory_space=pl.ANY),
            pl.BlockSpec(memory_space=pl.ANY),
            pl.BlockSpec((None, tc, d), lambda b, i: (b, i, 0)),
            pl.BlockSpec((tc, TOP_K), lambda b, i: (b * nct + i, 0)),
            pl.BlockSpec((None, 6, d), lambda b, i: (b, 0, 0)),
            pl.BlockSpec((1, d), lambda b, i: (0, 0)),
            pl.BlockSpec((1, d), lambda b, i: (0, 0)),
        ],
        out_specs=pl.BlockSpec((None, tc, d), lambda b, i: (b, i, 0)),
        out_shape=jax.ShapeDtypeStruct((bsz, s, d), F32),
        scratch_shapes=[
            pltpu.SMEM((2, TOP_K * tc), jnp.int32),
            pltpu.VMEM((2, TOP_K, tc * ROW_TILE, 128), F32),
            pltpu.SemaphoreType.DMA((2,)),
            pltpu.SemaphoreType.DMA((2,)),
        ],
        compiler_params=_cparams(("arbitrary", "arbitrary")),
        name="moe_combine",
    )(dest_t, y_rows, x, gates_t, mods_i, ln_g.reshape(1, d), ln_b.reshape(1, d))
    return out


def kernel(x, c, ada_w, ada_b, ln_g, ln_b, mla_w_dqkv, mla_g_q, mla_g_kv, mla_w_uq, mla_w_ukv, mla_w_o,
           pool_w, pool_scale, gdn_w_in, gdn_w_conv, gdn_a_log, gdn_dt_bias, gdn_g_norm, gdn_w_o,
           moe_w_router, moe_b_router, moe_w_gate_up, moe_b_gate_up, moe_w_down, moe_b_down):
    depth = ada_w.shape[0]
    alpha = (2.0 * depth) ** 0.25
    mods = _ada_mods(c, ada_w, ada_b)
    for i in range(depth):
        kind, j = i % 3, i // 3
        m_i = mods[i]
        if kind == 0:
            x = _mla_layer(x, m_i, mla_w_dqkv[j], mla_g_q[j], mla_g_kv[j], mla_w_uq[j], mla_w_ukv[j], mla_w_o[j],
                           ln_g[i, 0], ln_b[i, 0], alpha)
        elif kind == 1:
            x = _pool_layer(x, m_i, pool_w[j], pool_scale[j], ln_g[i, 0], ln_b[i, 0], alpha)
        else:
            x = _gdn_layer(x, m_i, gdn_w_in[j], gdn_w_conv[j], gdn_a_log[j], gdn_dt_bias[j], gdn_g_norm[j],
                           gdn_w_o[j], ln_g[i, 0], ln_b[i, 0], alpha)
        x = _moe_layer(x, m_i, moe_w_router[i], moe_b_router[i], moe_w_gate_up, moe_b_gate_up,
                       moe_w_down, moe_b_down, ln_g[i, 1], ln_b[i, 1], alpha, layer=i)
    return x
```

```python
import functools
import math

import jax
import jax.numpy as jnp
from jax import lax
from jax.experimental import pallas as pl
from jax.experimental.pallas import tpu as pltpu

F32 = jnp.float32
BF16 = jnp.bfloat16

CHUNK = 64
LN_EPS = 1e-5
RMS_EPS = 1e-6

MLA_HEADS = 8
MLA_NOPE = 128
MLA_ROPE = 64
MLA_V = 128
MLA_Q_LORA = 384
MLA_KV_LORA = 256
ROPE_THETA = 10000.0
MLA_QK = 256

POOL_WINDOWS = (2, 4, 8, 16)
POOL_HALO = 16

GDN_QK_HEADS = 8
GDN_V_HEADS = 16
GDN_DK = 128
GDN_DV = 128
GDN_CONV = 4
GDN_Q_DIM = GDN_QK_HEADS * GDN_DK
GDN_V_DIM = GDN_V_HEADS * GDN_DV
GDN_CONV_DIM = 2 * GDN_Q_DIM + GDN_V_DIM
GDN_HALO = 8

N_EXPERTS = 32
TOP_K = 4
SWIGLU_LIMIT = 7.0
SWIGLU_ALPHA = 1.702

VMEM_LIMIT = 52 * 1024 * 1024
ROW_TILE = 8


def _cparams(sem):
    return pltpu.CompilerParams(dimension_semantics=sem, vmem_limit_bytes=VMEM_LIMIT)


def _bdot(a, b):
    return jnp.dot(a.astype(BF16), b.astype(BF16), preferred_element_type=F32)


def _bdot_nt(a, b):
    return lax.dot_general(a.astype(BF16), b.astype(BF16), (((1,), (1,)), ((), ())),
                           preferred_element_type=F32)


def _bdot_tn(a, b):
    return lax.dot_general(a.astype(BF16), b.astype(BF16), (((0,), (0,)), ((), ())),
                           preferred_element_type=F32)


def _split(a):
    hi = a.astype(BF16)
    lo = (a - hi.astype(F32)).astype(BF16)
    return hi, lo


def _dot3(a, b):
    ah, al = _split(a)
    bh, bl = _split(b)
    d = functools.partial(jnp.dot, preferred_element_type=F32)
    return d(ah, bh) + (d(ah, bl) + d(al, bh))


def _dot3_nt(a, b):
    ah, al = _split(a)
    bh, bl = _split(b)
    d = lambda x, y: lax.dot_general(x, y, (((1,), (1,)), ((), ())), preferred_element_type=F32)
    return d(ah, bh) + (d(ah, bl) + d(al, bh))


def _layer_norm(z, g, b):
    mu = jnp.mean(z, axis=-1, keepdims=True)
    zc = z - mu
    var = jnp.mean(zc * zc, axis=-1, keepdims=True)
    return zc * lax.rsqrt(var + LN_EPS) * g + b


def _sigmoid(x):
    return 1.0 / (1.0 + jnp.exp(-x))


def _silu(x):
    return x * _sigmoid(x)


def _ada_kernel(c_ref, w_ref, b_ref, o_ref):
    sc = _silu(c_ref[...])
    o_ref[...] = _dot3(sc, w_ref[...]) + b_ref[...]


def _ada_mods(c, ada_w, ada_b):
    depth, d, n = ada_w.shape
    bsz = c.shape[0]
    rows = 8
    tn = 2048
    c8 = jnp.zeros((rows, d), F32).at[:bsz].set(c)
    out = pl.pallas_call(
        _ada_kernel,
        grid=(depth, n // tn),
        in_specs=[
            pl.BlockSpec((rows, d), lambda l, j: (0, 0)),
            pl.BlockSpec((None, d, tn), lambda l, j: (l, 0, j)),
            pl.BlockSpec((None, 1, tn), lambda l, j: (l, 0, j)),
        ],
        out_specs=pl.BlockSpec((None, rows, tn), lambda l, j: (l, 0, j)),
        out_shape=jax.ShapeDtypeStruct((depth, rows, n), F32),
        compiler_params=_cparams(("arbitrary", "arbitrary")),
        name="ada_mods",
    )(c8, ada_w, ada_b.reshape(depth, 1, n))
    return out[:, :bsz].reshape(depth, bsz, 6, d)


def _post_kernel(alpha, gate_row, y_ref, w_ref, x_ref, mod_ref, g_ref, b_ref, o_ref):
    y = jnp.dot(y_ref[...], w_ref[...], preferred_element_type=F32)
    gate = mod_ref[gate_row:gate_row + 1, :]
    z = alpha * x_ref[...] + (1.0 + gate) * y
    o_ref[...] = _layer_norm(z, g_ref[...], b_ref[...])


def _post(y_pre, w_bf, x, mods_i, ln_g, ln_b, alpha, gate_row, tm=512):
    bsz, s, d = x.shape
    kd = y_pre.shape[-1]
    return pl.pallas_call(
        functools.partial(_post_kernel, alpha, gate_row),
        grid=(bsz, s // tm),
        in_specs=[
            pl.BlockSpec((None, tm, kd), lambda b, i: (b, i, 0)),
            pl.BlockSpec((kd, d), lambda b, i: (0, 0)),
            pl.BlockSpec((None, tm, d), lambda b, i: (b, i, 0)),
            pl.BlockSpec((None, 6, d), lambda b, i: (b, 0, 0)),
            pl.BlockSpec((1, d), lambda b, i: (0, 0)),
            pl.BlockSpec((1, d), lambda b, i: (0, 0)),
        ],
        out_specs=pl.BlockSpec((None, tm, d), lambda b, i: (b, i, 0)),
        out_shape=jax.ShapeDtypeStruct((bsz, s, d), F32),
        compiler_params=_cparams(("arbitrary", "arbitrary")),
        name="post_proj_ln",
    )(y_pre, w_bf, x, mods_i, ln_g.reshape(1, d), ln_b.reshape(1, d))


def _mla_proj_kernel(qscale, x_ref, mod_ref, wd_ref, gq_ref, gkv_ref, wuq_ref, wukv_ref, cos_ref, sin_ref,
                     q_ref, k_ref, v_ref):
    shift = mod_ref[0:1, :]
    scale = mod_ref[1:2, :]
    h = x_ref[...] * (1.0 + scale) + shift
    down = _bdot(h, wd_ref[...])
    cq = down[:, :MLA_Q_LORA]
    ckv = down[:, MLA_Q_LORA:MLA_Q_LORA + MLA_KV_LORA]
    kra = down[:, 640:768]
    krb = down[:, 768:896]
    cos = cos_ref[...]
    sin = sin_ref[...]
    k_rope = (kra * cos + krb * sin).astype(BF16)

    cqn = cq * lax.rsqrt(jnp.mean(cq * cq, axis=-1, keepdims=True) + RMS_EPS) * gq_ref[...]
    ckvn = ckv * lax.rsqrt(jnp.mean(ckv * ckv, axis=-1, keepdims=True) + RMS_EPS) * gkv_ref[...]
    cqn = cqn.astype(BF16)
    ckvn = ckvn.astype(BF16)
    nh = MLA_HEADS
    for hh in range(nh):
        qn = jnp.dot(cqn, wuq_ref[:, hh * 128:(hh + 1) * 128], preferred_element_type=F32)
        qa = jnp.dot(cqn, wuq_ref[:, (nh + hh) * 128:(nh + hh + 1) * 128], preferred_element_type=F32)
        qb = jnp.dot(cqn, wuq_ref[:, (2 * nh + hh) * 128:(2 * nh + hh + 1) * 128], preferred_element_type=F32)
        q_ref[hh, :, 0:128] = (qn * qscale).astype(BF16)
        q_ref[hh, :, 128:256] = ((qa * cos + qb * sin) * qscale).astype(BF16)
        kv = jnp.dot(ckvn, wukv_ref[:, hh * 256:(hh + 1) * 256], preferred_element_type=F32)
        k_ref[hh, :, 0:128] = kv[:, :128].astype(BF16)
        k_ref[hh, :, 128:256] = k_rope
        v_ref[hh, :, :] = kv[:, 128:].astype(BF16)


MLA_HPS = 2


def _mla_attn_kernel(tq, tk, q_ref, k_ref, v_ref, o_ref):
    i = pl.program_id(2)
    nhs = MLA_HPS
    qs = [q_ref[h] for h in range(nhs)]

    def block(j, carry, masked):
        start = pl.multiple_of(j * tk, tk)
        ss = [lax.dot_general(qs[h], k_ref[h, pl.ds(start, tk), :], (((1,), (1,)), ((), ())),
                              preferred_element_type=F32) for h in range(nhs)]
        if masked:
            qc = (i * tq + lax.broadcasted_iota(jnp.int32, (tq, tk), 0)) // CHUNK
            kc = (j * tk + lax.broadcasted_iota(jnp.int32, (tq, tk), 1)) // CHUNK
            ss = [jnp.where(kc <= qc, s, -1e30) for s in ss]
        m_new = [jnp.maximum(carry[h][0], jnp.max(ss[h], axis=-1, keepdims=True)) for h in range(nhs)]
        ps = [jnp.exp(ss[h] - m_new[h]) for h in range(nhs)]
        corr = [jnp.exp(carry[h][0] - m_new[h]) for h in range(nhs)]
        l_new = [corr[h] * carry[h][1] + jnp.sum(ps[h], axis=-1, keepdims=True) for h in range(nhs)]
        pv = [jnp.dot(ps[h].astype(BF16), v_ref[h, pl.ds(start, tk), :], preferred_element_type=F32)
              for h in range(nhs)]
        return tuple((m_new[h], l_new[h], corr[h] * carry[h][2] + pv[h]) for h in range(nhs))

    init = tuple((jnp.full((tq, 1), -1e30, F32), jnp.zeros((tq, 1), F32), jnp.zeros((tq, MLA_V), F32))
                 for _ in range(nhs))
    carry = lax.fori_loop(0, i, lambda j, c: block(j, c, False), init)
    fin = block(i, carry, True)
    for h in range(nhs):
        o_ref[:, h * MLA_V:(h + 1) * MLA_V] = (fin[h][2] / fin[h][1]).astype(BF16)


def _mla_layer(x, mods_i, w_dqkv, g_q, g_kv, w_uq, w_ukv, w_o, ln_g, ln_b, alpha, tm=512, tq=512):
    bsz, s, d = x.shape
    nh = MLA_HEADS
    half = MLA_ROPE // 2
    k1 = w_dqkv[:, 640:640 + half]
    k2 = w_dqkv[:, 640 + half:704]
    z64 = jnp.zeros((d, 64), F32)
    wd = jnp.concatenate([w_dqkv[:, :640], k1, k2, z64, k2, k1, z64], axis=1).astype(BF16)
    wq = w_uq.reshape(MLA_Q_LORA, nh, MLA_NOPE + MLA_ROPE)
    wq_n = wq[:, :, :MLA_NOPE].reshape(MLA_Q_LORA, nh * 128)
    x1 = wq[:, :, MLA_NOPE:MLA_NOPE + half]
    x2 = wq[:, :, MLA_NOPE + half:]
    zq = jnp.zeros((MLA_Q_LORA, nh, 64), F32)
    wq_a = jnp.concatenate([x1, x2, zq], axis=2).reshape(MLA_Q_LORA, nh * 128)
    wq_b = jnp.concatenate([x2, x1, zq], axis=2).reshape(MLA_Q_LORA, nh * 128)
    wuq = jnp.concatenate([wq_n, wq_a, wq_b], axis=1).astype(BF16)
    wukv = w_ukv.astype(BF16)
    pos = jnp.arange(s, dtype=F32)
    inv_freq = ROPE_THETA ** (-jnp.arange(0, MLA_ROPE, 2, dtype=F32) / MLA_ROPE)
    ang = pos[:, None] * inv_freq[None, :]
    cs, sn = jnp.cos(ang), jnp.sin(ang)
    zt = jnp.zeros((s, 64), F32)
    cos_t = jnp.concatenate([cs, cs, zt], axis=1)
    sin_t = jnp.concatenate([-sn, sn, zt], axis=1)
    qscale = (MLA_NOPE + MLA_ROPE) ** -0.5

    q, k, v = pl.pallas_call(
        functools.partial(_mla_proj_kernel, qscale),
        grid=(bsz, s // tm),
        in_specs=[
            pl.BlockSpec((None, tm, d), lambda b, i: (b, i, 0)),
            pl.BlockSpec((None, 6, d), lambda b, i: (b, 0, 0)),
            pl.BlockSpec(wd.shape, lambda b, i: (0, 0)),
            pl.BlockSpec((1, MLA_Q_LORA), lambda b, i: (0, 0)),
            pl.BlockSpec((1, MLA_KV_LORA), lambda b, i: (0, 0)),
            pl.BlockSpec(wuq.shape, lambda b, i: (0, 0)),
            pl.BlockSpec(wukv.shape, lambda b, i: (0, 0)),
            pl.BlockSpec((tm, 128), lambda b, i: (i, 0)),
            pl.BlockSpec((tm, 128), lambda b, i: (i, 0)),
        ],
        out_specs=[
            pl.BlockSpec((None, nh, tm, MLA_QK), lambda b, i: (b, 0, i, 0)),
            pl.BlockSpec((None, nh, tm, MLA_QK), lambda b, i: (b, 0, i, 0)),
            pl.BlockSpec((None, nh, tm, MLA_V), lambda b, i: (b, 0, i, 0)),
        ],
        out_shape=[
            jax.ShapeDtypeStruct((bsz, nh, s, MLA_QK), BF16),
            jax.ShapeDtypeStruct((bsz, nh, s, MLA_QK), BF16),
            jax.ShapeDtypeStruct((bsz, nh, s, MLA_V), BF16),
        ],
        compiler_params=_cparams(("arbitrary", "arbitrary")),
        name="mla_proj",
    )(x, mods_i, wd, g_q.reshape(1, -1), g_kv.reshape(1, -1), wuq, wukv, cos_t, sin_t)

    o = pl.pallas_call(
        functools.partial(_mla_attn_kernel, tq, tq),
        grid=(bsz, nh // MLA_HPS, s // tq),
        in_specs=[
            pl.BlockSpec((None, MLA_HPS, tq, MLA_QK), lambda b, h, i: (b, h, i, 0)),
            pl.BlockSpec((None, MLA_HPS, s, MLA_QK), lambda b, h, i: (b, h, 0, 0)),
            pl.BlockSpec((None, MLA_HPS, s, MLA_V), lambda b, h, i: (b, h, 0, 0)),
        ],
        out_specs=pl.BlockSpec((None, tq, MLA_HPS * MLA_V), lambda b, h, i: (b, i, h)),
        out_shape=jax.ShapeDtypeStruct((bsz, s, nh * MLA_V), BF16),
        compiler_params=_cparams(("arbitrary", "arbitrary", "arbitrary")),
        name="mla_attn",
    )(q, k, v)

    return _post(o, w_o.astype(BF16), x, mods_i, ln_g, ln_b, alpha, gate_row=2)


def _pool_kernel(alpha, tm, x_ref, halo_ref, mod_ref, w_ref, cs_ref, g_ref, b_ref, o_ref, buf):
    i = pl.program_id(1)
    d = x_ref.shape[-1]
    gd = d // len(POOL_WINDOWS)
    shift = mod_ref[0:1, :]
    scale = mod_ref[1:2, :]
    gate = mod_ref[2:3, :]
    x = x_ref[...]
    h = x * (1.0 + scale) + shift
    hh = halo_ref[...] * (1.0 + scale) + shift
    hh = jnp.where(i > 0, hh, 0.0)
    buf[0:POOL_HALO, :] = hh
    buf[POOL_HALO:, :] = h
    t = i * tm + lax.broadcasted_iota(jnp.int32, (tm, 1), 0)
    ys = []
    for g, win in enumerate(POOL_WINDOWS):
        cols = slice(g * gd, (g + 1) * gd)
        acc = buf[POOL_HALO:POOL_HALO + tm, cols]
        for j in range(1, win):
            acc = acc + buf[POOL_HALO - j:POOL_HALO - j + tm, cols]
        count = jnp.minimum(t + 1, win).astype(F32)
        dg = acc / count - h[:, cols]
        ys.append(_bdot(dg, w_ref[g]))
    y = jnp.concatenate(ys, axis=-1) * cs_ref[...]
    z = alpha * x + (1.0 + gate) * y
    o_ref[...] = _layer_norm(z, g_ref[...], b_ref[...])


def _pool_layer(x, mods_i, w_pool, ch_scale, ln_g, ln_b, alpha, tm=512):
    bsz, s, d = x.shape
    gd = d // len(POOL_WINDOWS)
    hb = tm // POOL_HALO
    return pl.pallas_call(
        functools.partial(_pool_kernel, alpha, tm),
        grid=(bsz, s // tm),
        in_specs=[
            pl.BlockSpec((None, tm, d), lambda b, i: (b, i, 0)),
            pl.BlockSpec((None, POOL_HALO, d), lambda b, i: (b, jnp.maximum(i * hb - 1, 0), 0)),
            pl.BlockSpec((None, 6, d), lambda b, i: (b, 0, 0)),
            pl.BlockSpec((len(POOL_WINDOWS), gd, gd), lambda b, i: (0, 0, 0)),
            pl.BlockSpec((1, d), lambda b, i: (0, 0)),
            pl.BlockSpec((1, d), lambda b, i: (0, 0)),
            pl.BlockSpec((1, d), lambda b, i: (0, 0)),
        ],
        out_specs=pl.BlockSpec((None, tm, d), lambda b, i: (b, i, 0)),
        out_shape=jax.ShapeDtypeStruct((bsz, s, d), F32),
        scratch_shapes=[pltpu.VMEM((tm + POOL_HALO, d), F32)],
        compiler_params=_cparams(("arbitrary", "arbitrary")),
        name="pool_layer",
    )(x, x, mods_i, w_pool.astype(BF16), ch_scale.reshape(1, d), ln_g.reshape(1, d), ln_b.reshape(1, d))


def _gdn_proj_kernel(tm, x_ref, halo_ref, mod_ref, wqkv_ref, wz_ref, wba_ref, wc_ref, alog_ref, dtb_ref,
                     q_ref, k_ref, v_ref, z_ref, beta_ref, gc_ref, hbuf, pbuf):
    i = pl.program_id(1)
    shift = mod_ref[0:1, :]
    scale = mod_ref[1:2, :]
    h = x_ref[...] * (1.0 + scale) + shift
    hh = halo_ref[...] * (1.0 + scale) + shift
    hh = jnp.where(i > 0, hh, 0.0)
    hbuf[0:GDN_HALO, :] = hh
    hbuf[GDN_HALO:, :] = h
    h_ext = hbuf[...].astype(BF16)

    z_ref[...] = jnp.dot(h.astype(BF16), wz_ref[...], preferred_element_type=F32).astype(BF16)

    ba = _dot3(h, wba_ref[...])
    nv = GDN_V_HEADS
    beta_ref[...] = _sigmoid(ba[:, :nv])
    a = ba[:, nv:] + dtb_ref[...]
    softplus = jnp.maximum(a, 0.0) + jnp.log(1.0 + jnp.exp(-jnp.abs(a)))
    g = -jnp.exp(alog_ref[...]) * softplus
    r = lax.broadcasted_iota(jnp.int32, (tm, tm), 0)
    c = lax.broadcasted_iota(jnp.int32, (tm, tm), 1)
    tri = jnp.where((c <= r) & (c // CHUNK == r // CHUNK), 1.0, 0.0).astype(BF16)
    g_hi = g.astype(BF16)
    g_lo = (g - g_hi.astype(F32)).astype(BF16)
    g_lo2 = (g - g_hi.astype(F32) - g_lo.astype(F32)).astype(BF16)
    d = functools.partial(jnp.dot, preferred_element_type=F32)
    gc_ref[...] = d(tri, g_hi) + (d(tri, g_lo) + d(tri, g_lo2))

    cw = 512
    for cb in range(GDN_CONV_DIM // cw):
        cols = slice(cb * cw, (cb + 1) * cw)
        pbuf[...] = jnp.dot(h_ext, wqkv_ref[:, cols], preferred_element_type=F32)
        acc = pbuf[GDN_HALO:GDN_HALO + tm, :] * wc_ref[GDN_CONV - 1:GDN_CONV, cols]
        for j in range(GDN_CONV - 1):
            off = GDN_HALO - (GDN_CONV - 1) + j
            acc = acc + pbuf[off:off + tm, :] * wc_ref[j:j + 1, cols]
        y = _silu(acc)
        if cb * cw < 2 * GDN_Q_DIM:
            outs = []
            for hd in range(cw // GDN_DK):
                yh = y[:, hd * GDN_DK:(hd + 1) * GDN_DK]
                outs.append(yh * lax.rsqrt(jnp.sum(yh * yh, axis=-1, keepdims=True) + RMS_EPS))
            y = jnp.concatenate(outs, axis=-1)
            if cb * cw < GDN_Q_DIM:
                q_ref[:, cols] = y * (GDN_DK ** -0.5)
            else:
                k_ref[:, cb * cw - GDN_Q_DIM:(cb + 1) * cw - GDN_Q_DIM] = y
        else:
            v_ref[:, cb * cw - 2 * GDN_Q_DIM:(cb + 1) * cw - 2 * GDN_Q_DIM] = y


def _col(x, hidx):
    lane = lax.broadcasted_iota(jnp.int32, x.shape, 1)
    return jnp.sum(jnp.where(lane == hidx, x, 0.0), axis=-1, keepdims=True)


GDN_HPS = 4


def _tile_rows(x, n):
    return jnp.concatenate([x] * n, axis=0)


def _gdn_chunk_kernel(tb, q_ref, k_ref, v_ref, z_ref, beta_ref, gc_ref, gr_ref, gn_ref, o_ref, s01, s23):
    hg = pl.program_id(1)
    n = pl.program_id(2)
    cs = CHUNK
    hp = GDN_HPS
    dv = GDN_DV
    wp = hp * cs
    wv = hp * dv

    @pl.when(n == 0)
    def _():
        s01[...] = jnp.zeros_like(s01)
        s23[...] = jnp.zeros_like(s23)

    bcols = [_col(beta_ref[...], hp * hg + i) for i in range(hp)]
    gcols = [_col(gc_ref[...], hp * hg + i) for i in range(hp)]
    beta_b = jnp.concatenate([jnp.broadcast_to(cc, (tb, dv)) for cc in bcols], axis=1)
    g_b = jnp.concatenate([jnp.broadcast_to(cc, (tb, dv)) for cc in gcols], axis=1)
    lane_p = lax.broadcasted_iota(jnp.int32, (tb, wp), 1)
    g_p = jnp.broadcast_to(gcols[hp - 1], (tb, wp))
    for i in reversed(range(hp - 1)):
        g_p = jnp.where(lane_p < (i + 1) * cs, gcols[i], g_p)

    r = lax.broadcasted_iota(jnp.int32, (cs, wp), 0)
    lc = lax.broadcasted_iota(jnp.int32, (cs, wp), 1) & (cs - 1)
    tril_incl = r >= lc
    tril_strict = r > lc
    eye_p = jnp.where(r == lc, 1.0, 0.0).astype(F32)
    rr = lax.broadcasted_iota(jnp.int32, (wp, wp), 0)
    ll = lax.broadcasted_iota(jnp.int32, (wp, wp), 1)
    bd_mask = (rr // cs) == (ll // cs)
    pair_mask = (rr // dv) == (ll // dv)
    rv = lax.broadcasted_iota(jnp.int32, (wp, wv), 0)
    lv = lax.broadcasted_iota(jnp.int32, (wp, wv), 1)
    bdv_mask = (rv // cs) == (lv // dv)

    def bd(y):
        return jnp.where(bd_mask, _tile_rows(y, hp), 0.0).astype(BF16)

    def bdv(y):
        return jnp.where(bdv_mask, _tile_rows(y, hp), 0.0).astype(BF16)

    def mm(a, b):
        return jnp.dot(a.astype(BF16), b, preferred_element_type=F32)

    def mm_nt(a, b):
        return lax.dot_general(a.astype(BF16), b, (((1,), (1,)), ((), ())), preferred_element_type=F32)

    states = [s01[...], s23[...]]
    gn = jnp.concatenate([gn_ref[...]] * hp, axis=1)
    nch = tb // cs
    pre = []
    for c in range(nch):
        rows = slice(c * cs, (c + 1) * cs)
        q2 = q_ref[rows, :]
        k2 = k_ref[rows, :]
        q4 = jnp.concatenate([q2[:, :dv], q2[:, :dv], q2[:, dv:], q2[:, dv:]], axis=1)
        k4 = jnp.concatenate([k2[:, :dv], k2[:, :dv], k2[:, dv:], k2[:, dv:]], axis=1)
        v4 = v_ref[rows, :]
        bb = beta_b[rows, :]
        gb = g_b[rows, :]
        gp = g_p[rows, :]
        grow = gr_ref[c:c + 1, :]
        decay = jnp.exp(jnp.where(tril_incl, gp - grow, -jnp.inf))
        eg = jnp.exp(gb)
        glast = gb[cs - 1:cs, :]
        kb = k4 * bb
        vb = v4 * bb
        kbd = bdv(k4)
        kk = mm_nt(kb, kbd)
        xm = -jnp.where(tril_strict, kk * decay, 0.0)
        a_intra = mm_nt(q4, kbd) * decay
        pre.append(dict(vbd=bdv(vb), kgd=bdv(kb * eg), a=a_intra, qg=q4 * eg, kd=k4 * jnp.exp(glast - gb),
                        gl=jnp.exp(glast), xm=xm))
    tinvs = [eye_p + pc["xm"] for pc in pre]
    pws = [pc["xm"] for pc in pre]
    bdps = [bd(pw) for pw in pws]
    for _ in range(5):
        pws = [mm(pw, bdp) for pw, bdp in zip(pws, bdps)]
        bdps = [bd(pw) for pw in pws]
        tinvs = [tinv + mm(tinv, bdp) for tinv, bdp in zip(tinvs, bdps)]
    us = [mm(tinv, pc["vbd"]) for tinv, pc in zip(tinvs, pre)]
    ws = [mm(tinv, pc["kgd"]) for tinv, pc in zip(tinvs, pre)]
    for c in range(nch):
        rows = slice(c * cs, (c + 1) * cs)
        u, w, a_intra, qg, kd, gl = us[c], ws[c], pre[c]["a"], pre[c]["qg"], pre[c]["kd"], pre[c]["gl"]
        vn, oq = [], []
        for p in range(2):
            sl = slice(p * 2 * dv, (p + 1) * 2 * dv)
            sb = states[p].astype(BF16)
            vn.append(u[:, sl] - mm(w[:, sl], sb))
            oq.append(mm(qg[:, sl], sb))
        v_new = jnp.concatenate(vn, axis=1)
        o = jnp.concatenate(oq, axis=1) + mm(a_intra, bdv(v_new))
        for p in range(2):
            sl = slice(p * 2 * dv, (p + 1) * 2 * dv)
            upd = lax.dot_general(kd[:, sl].astype(BF16), v_new[:, sl].astype(BF16), (((0,), (0,)), ((), ())),
                                  preferred_element_type=F32)
            states[p] = states[p] * gl[:, sl] + jnp.where(pair_mask, upd, 0.0)
        outs = []
        for i in range(hp):
            oi = o[:, i * dv:(i + 1) * dv]
            outs.append(oi * lax.rsqrt(jnp.mean(oi * oi, axis=-1, keepdims=True) + RMS_EPS))
        on = jnp.concatenate(outs, axis=1) * gn
        zz = z_ref[rows, :].astype(F32)
        o_ref[rows, :] = (on * _silu(zz)).astype(BF16)
    s01[...] = states[0]
    s23[...] = states[1]


def _gdn_layer(x, mods_i, w_in, w_conv, a_log, dt_bias, g_norm, w_o, ln_g, ln_b, alpha, tm=256, tb=512):
    bsz, s, d = x.shape
    nv = GDN_V_HEADS
    wqkv = w_in[:, :GDN_CONV_DIM].astype(BF16)
    wz = w_in[:, GDN_CONV_DIM:GDN_CONV_DIM + GDN_V_DIM].astype(BF16)
    wba = w_in[:, GDN_CONV_DIM + GDN_V_DIM:]
    hb = tm // GDN_HALO
    q, k, v, z, beta, gc = pl.pallas_call(
        functools.partial(_gdn_proj_kernel, tm),
        grid=(bsz, s // tm),
        in_specs=[
            pl.BlockSpec((None, tm, d), lambda b, i: (b, i, 0)),
            pl.BlockSpec((None, GDN_HALO, d), lambda b, i: (b, jnp.maximum(i * hb - 1, 0), 0)),
            pl.BlockSpec((None, 6, d), lambda b, i: (b, 0, 0)),
            pl.BlockSpec(wqkv.shape, lambda b, i: (0, 0)),
            pl.BlockSpec(wz.shape, lambda b, i: (0, 0)),
            pl.BlockSpec(wba.shape, lambda b, i: (0, 0)),
            pl.BlockSpec((GDN_CONV, GDN_CONV_DIM), lambda b, i: (0, 0)),
            pl.BlockSpec((1, nv), lambda b, i: (0, 0)),
            pl.BlockSpec((1, nv), lambda b, i: (0, 0)),
        ],
        out_specs=[
            pl.BlockSpec((None, tm, GDN_Q_DIM), lambda b, i: (b, i, 0)),
            pl.BlockSpec((None, tm, GDN_Q_DIM), lambda b, i: (b, i, 0)),
            pl.BlockSpec((None, tm, GDN_V_DIM), lambda b, i: (b, i, 0)),
            pl.BlockSpec((None, tm, GDN_V_DIM), lambda b, i: (b, i, 0)),
            pl.BlockSpec((None, tm, nv), lambda b, i: (b, i, 0)),
            pl.BlockSpec((None, tm, nv), lambda b, i: (b, i, 0)),
        ],
        out_shape=[
            jax.ShapeDtypeStruct((bsz, s, GDN_Q_DIM), F32),
            jax.ShapeDtypeStruct((bsz, s, GDN_Q_DIM), F32),
            jax.ShapeDtypeStruct((bsz, s, GDN_V_DIM), F32),
            jax.ShapeDtypeStruct((bsz, s, GDN_V_DIM), BF16),
            jax.ShapeDtypeStruct((bsz, s, nv), F32),
            jax.ShapeDtypeStruct((bsz, s, nv), F32),
        ],
        scratch_shapes=[pltpu.VMEM((tm + GDN_HALO, d), F32), pltpu.VMEM((tm + GDN_HALO, 512), F32)],
        compiler_params=_cparams(("arbitrary", "arbitrary")),
        name="gdn_proj",
    )(x, x, mods_i, wqkv, wz, wba, w_conv, a_log.reshape(1, nv), dt_bias.reshape(1, nv))

    hp = GDN_HPS
    nc = s // CHUNK
    gr = gc.reshape(bsz, nc, CHUNK, nv // hp, hp).transpose(0, 3, 1, 4, 2).reshape(bsz, nv // hp, nc, hp * CHUNK)
    rep = GDN_V_HEADS // GDN_QK_HEADS
    cpb = tb // CHUNK
    qw = hp // rep * GDN_DK
    o = pl.pallas_call(
        functools.partial(_gdn_chunk_kernel, tb),
        grid=(bsz, nv // hp, s // tb),
        in_specs=[
            pl.BlockSpec((None, tb, qw), lambda b, h, n: (b, n, h)),
            pl.BlockSpec((None, tb, qw), lambda b, h, n: (b, n, h)),
            pl.BlockSpec((None, tb, hp * GDN_DV), lambda b, h, n: (b, n, h)),
            pl.BlockSpec((None, tb, hp * GDN_DV), lambda b, h, n: (b, n, h)),
            pl.BlockSpec((None, tb, nv), lambda b, h, n: (b, n, 0)),
            pl.BlockSpec((None, tb, nv), lambda b, h, n: (b, n, 0)),
            pl.BlockSpec((None, None, cpb, hp * CHUNK), lambda b, h, n: (b, h, n, 0)),
            pl.BlockSpec((1, GDN_DV), lambda b, h, n: (0, 0)),
        ],
        out_specs=pl.BlockSpec((None, tb, hp * GDN_DV), lambda b, h, n: (b, n, h)),
        out_shape=jax.ShapeDtypeStruct((bsz, s, GDN_V_DIM), BF16),
        scratch_shapes=[pltpu.VMEM((2 * GDN_DK, 2 * GDN_DV), F32), pltpu.VMEM((2 * GDN_DK, 2 * GDN_DV), F32)],
        compiler_params=_cparams(("arbitrary", "arbitrary", "arbitrary")),
        name="gdn_chunk",
    )(q, k, v, z, beta, gc, gr, g_norm.reshape(1, GDN_DV))

    return _post(o, w_o.astype(BF16), x, mods_i, ln_g, ln_b, alpha, gate_row=2)


def _router_kernel(tm, x_ref, mod_ref, wr_ref, br_ref, idx_ref, gate_ref, rank_ref, cnt_ref, run):
    step = pl.program_id(0) * pl.num_programs(1) + pl.program_id(1)

    @pl.when(step == 0)
    def _():
        run[...] = jnp.zeros_like(run)

    shift = mod_ref[3:4, :]
    scale = mod_ref[4:5, :]
    h = x_ref[...] * (1.0 + scale) + shift
    logits = _dot3_nt(wr_ref[...], h) + br_ref[...]
    eid = lax.broadcasted_iota(jnp.int32, (N_EXPERTS, tm), 0)
    vals, idxs, hots = [], [], []
    cur = logits
    for _ in range(TOP_K):
        m = jnp.max(cur, axis=0, keepdims=True)
        sel = jnp.min(jnp.where(cur == m, eid, N_EXPERTS), axis=0, keepdims=True)
        hot = eid == sel
        vals.append(m)
        idxs.append(sel)
        hots.append(hot)
        cur = jnp.where(hot, -jnp.inf, cur)
    es = [jnp.exp(v - vals[0]) for v in vals]
    den = es[0] + es[1] + es[2] + es[3]
    cnt = jnp.zeros((N_EXPERTS, tm), F32)
    for hot in hots:
        cnt = cnt + jnp.where(hot, 1.0, 0.0)
    r = lax.broadcasted_iota(jnp.int32, (tm, tm), 0)
    c = lax.broadcasted_iota(jnp.int32, (tm, tm), 1)
    before = jnp.where(r < c, 1.0, 0.0).astype(BF16)
    prefix = jnp.dot(cnt.astype(BF16), before, preferred_element_type=F32) + run[:, 0:1]
    for kk in range(TOP_K):
        idx_ref[kk:kk + 1, :] = idxs[kk]
        gate_ref[kk:kk + 1, :] = es[kk] / den
        rank_ref[kk:kk + 1, :] = jnp.sum(jnp.where(hots[kk], prefix, 0.0), axis=0, keepdims=True).astype(jnp.int32)
    run[...] = run[...] + jnp.sum(cnt, axis=1, keepdims=True)
    cnt_ref[...] = run[...]


def _dispatch_kernel(tm, dest_hbm, x_ref, mod_ref, zeros_hbm, rows_hbm, d_smem, hbuf, d_sem, s_sem):
    del zeros_hbm
    step = pl.program_id(0) * pl.num_programs(1) + pl.program_id(1)
    nsteps = pl.num_programs(0) * pl.num_programs(1)
    slot = step % 2
    nxt = 1 - slot

    def d_copy(blk, sl):
        return pltpu.make_async_copy(dest_hbm.at[blk], d_smem.at[sl], d_sem.at[sl])

    def wait_scatter(sl):
        for _ in range(TOP_K):
            pltpu.make_async_copy(hbuf.at[sl], rows_hbm.at[pl.ds(0, tm * ROW_TILE), :], s_sem.at[sl]).wait()

    @pl.when(step == 0)
    def _():
        d_copy(0, 0).start()

    @pl.when(step >= 2)
    def _():
        wait_scatter(slot)

    shift = mod_ref[3:4, :]
    scale = mod_ref[4:5, :]
    h = x_ref[...] * (1.0 + scale) + shift
    for sc in range(ROW_TILE):
        hbuf[slot, pl.ds(sc, tm, stride=ROW_TILE), :] = h[:, sc * 128:(sc + 1) * 128]

    d_copy(step, slot).wait()

    @pl.when(step + 1 < nsteps)
    def _():
        d_copy(step + 1, nxt).start()

    def body(r, carry):
        src = hbuf.at[slot, pl.ds(pl.multiple_of(r * ROW_TILE, ROW_TILE), ROW_TILE), :]
        for kk in range(TOP_K):
            dst = d_smem[slot, kk * tm + r]
            out = rows_hbm.at[pl.ds(pl.multiple_of(dst * ROW_TILE, ROW_TILE), ROW_TILE), :]
            pltpu.make_async_copy(src, out, s_sem.at[slot]).start(priority=kk % 2)
        return carry
    lax.fori_loop(0, tm, body, 0, unroll=4)

    @pl.when(step == nsteps - 1)
    def _():
        wait_scatter(slot)

        @pl.when(nsteps > 1)
        def _():
            wait_scatter(nxt)


def _expert_kernel(tm, be_ref, nu_ref, x_ref, wgu_ref, bgu_ref, wd_ref, bd_ref, y_ref, wgu_bf, wd_bf):
    j = pl.program_id(0)
    n_used = nu_ref[0]
    changed = jnp.logical_or(j == 0, be_ref[j] != be_ref[jnp.maximum(j - 1, 0)])

    @pl.when(jnp.logical_and(changed, j < n_used))
    def _():
        wgu_bf[...] = wgu_ref[...].astype(BF16)
        wd_bf[...] = wd_ref[...].astype(BF16)

    @pl.when(j >= n_used)
    def _():
        y_ref[...] = jnp.zeros_like(y_ref)

    @pl.when(j < n_used)
    def _():
        nsc = ROW_TILE
        xb = jnp.concatenate([x_ref[pl.ds(sc, tm, stride=nsc), :] for sc in range(nsc)], axis=-1).astype(BF16)
        gu = jnp.dot(xb, wgu_bf[...], preferred_element_type=F32) + bgu_ref[...]
        dh = gu.shape[-1] // 2
        gate = jnp.minimum(gu[:, :dh], SWIGLU_LIMIT)
        up = jnp.clip(gu[:, dh:], -SWIGLU_LIMIT, SWIGLU_LIMIT)
        act = gate * _sigmoid(SWIGLU_ALPHA * gate) * (up + 1.0)
        y = jnp.dot(act.astype(BF16), wd_bf[...], preferred_element_type=F32) + bd_ref[...]
        for sc in range(nsc):
            y_ref[pl.ds(sc, tm, stride=nsc), :] = y[:, sc * 128:(sc + 1) * 128]


def _combine_kernel(alpha, tm, dest_hbm, y_hbm, x_ref, gate_ref, mod_ref, g_ref, b_ref, o_ref,
                    d_smem, ybuf, d_sem, y_sem):
    step = pl.program_id(0) * pl.num_programs(1) + pl.program_id(1)
    nsteps = pl.num_programs(0) * pl.num_programs(1)
    slot = step % 2
    nxt = 1 - slot

    def d_copy(blk, sl):
        return pltpu.make_async_copy(dest_hbm.at[blk], d_smem.at[sl], d_sem.at[sl])

    def issue_rows(sl):
        def body(r, carry):
            for kk in range(TOP_K):
                dst = d_smem[sl, kk * tm + r]
                src = y_hbm.at[pl.ds(pl.multiple_of(dst * ROW_TILE, ROW_TILE), ROW_TILE), :]
                dbuf = ybuf.at[sl, kk, pl.ds(pl.multiple_of(r * ROW_TILE, ROW_TILE), ROW_TILE), :]
                pltpu.make_async_copy(src, dbuf, y_sem.at[sl]).start(priority=kk % 2)
            return carry
        lax.fori_loop(0, tm, body, 0, unroll=4)

    @pl.when(step == 0)
    def _():
        d_copy(0, 0).start()
        d_copy(0, 0).wait()
        issue_rows(0)

        @pl.when(nsteps > 1)
        def _():
            d_copy(1, 1).start()

    @pl.when(step + 1 < nsteps)
    def _():
        d_copy(step + 1, nxt).wait()
        issue_rows(nxt)

    @pl.when(step + 2 < nsteps)
    def _():
        d_copy(step + 2, slot).start()

    for kk in range(TOP_K):
        pltpu.make_async_copy(y_hbm.at[pl.ds(0, tm * ROW_TILE), :], ybuf.at[slot, kk], y_sem.at[slot]).wait()
    g = gate_ref[...]
    parts = []
    for sc in range(ROW_TILE):
        acc = ybuf[slot, 0, pl.ds(sc, tm, stride=ROW_TILE), :] * g[:, 0:1]
        for kk in range(1, TOP_K):
            acc = acc + ybuf[slot, kk, pl.ds(sc, tm, stride=ROW_TILE), :] * g[:, kk:kk + 1]
        parts.append(acc)
    y = jnp.concatenate(parts, axis=-1)
    gate_f = mod_ref[5:6, :]
    z = alpha * x_ref[...] + (1.0 + gate_f) * y
    o_ref[...] = _layer_norm(z, g_ref[...], b_ref[...])


def _moe_layer(x, mods_i, w_router, b_router, w_gate_up, b_gate_up, w_down, b_down, ln_g, ln_b, alpha,
               layer=0, tr=512, te=256, tc=256):
    bsz, s, d = x.shape
    assert d == ROW_TILE * 128, "token rows are stored as one (8,128) f32 tile"
    t = bsz * s
    ne = N_EXPERTS
    nst = s // tr
    idx, gates, rank, counts = pl.pallas_call(
        functools.partial(_router_kernel, tr),
        grid=(bsz, nst),
        in_specs=[
            pl.BlockSpec((None, tr, d), lambda b, i: (b, i, 0)),
            pl.BlockSpec((None, 6, d), lambda b, i: (b, 0, 0)),
            pl.BlockSpec((ne, d), lambda b, i: (0, 0)),
            pl.BlockSpec((ne, 1), lambda b, i: (0, 0)),
        ],
        out_specs=[
            pl.BlockSpec((TOP_K, tr), lambda b, i: (0, b * nst + i)),
            pl.BlockSpec((TOP_K, tr), lambda b, i: (0, b * nst + i)),
            pl.BlockSpec((TOP_K, tr), lambda b, i: (0, b * nst + i)),
            pl.BlockSpec((ne, 128), lambda b, i: (0, 0)),
        ],
        out_shape=[
            jax.ShapeDtypeStruct((TOP_K, t), jnp.int32),
            jax.ShapeDtypeStruct((TOP_K, t), F32),
            jax.ShapeDtypeStruct((TOP_K, t), jnp.int32),
            jax.ShapeDtypeStruct((ne, 128), F32),
        ],
        scratch_shapes=[pltpu.VMEM((ne, 128), F32)],
        compiler_params=_cparams(("arbitrary", "arbitrary")),
        name="moe_router",
    )(x, mods_i, w_router.T, b_router.reshape(ne, 1))

    cnt = counts[:, 0].astype(jnp.int32)
    padded = (cnt + te - 1) // te * te
    pend = jnp.cumsum(padded)
    pstart = pend - padded
    n_rows = t * TOP_K + ne * te
    n_blocks = n_rows // te
    eids = jnp.arange(ne, dtype=jnp.int32)
    dest = jnp.sum(jnp.where(idx[None] == eids[:, None, None], pstart[:, None, None], 0), axis=0) + rank
    n_used = (pend[-1] // te).astype(jnp.int32).reshape(1)
    blk_start = jnp.minimum(jnp.arange(n_blocks, dtype=jnp.int32) * te, pend[-1] - te)
    blk_expert = jnp.minimum(jnp.sum((blk_start[:, None] >= pend[None, :]).astype(jnp.int32), axis=1), ne - 1)
    blk_expert = blk_expert + layer * ne

    nct = s // tc
    dest_t = dest.reshape(TOP_K, t // tc, tc).transpose(1, 0, 2).reshape(t // tc, TOP_K * tc)
    x_rows = pl.pallas_call(
        functools.partial(_dispatch_kernel, tc),
        grid=(bsz, nct),
        in_specs=[
            pl.BlockSpec(memory_space=pl.ANY),
            pl.BlockSpec((None, tc, d), lambda b, i: (b, i, 0)),
            pl.BlockSpec((None, 6, d), lambda b, i: (b, 0, 0)),
            pl.BlockSpec(memory_space=pl.ANY),
        ],
        out_specs=pl.BlockSpec(memory_space=pl.ANY),
        scratch_shapes=[
            pltpu.SMEM((2, TOP_K * tc), jnp.int32),
            pltpu.VMEM((2, tc * ROW_TILE, 128), F32),
            pltpu.SemaphoreType.DMA((2,)),
            pltpu.SemaphoreType.DMA((2,)),
        ],
        out_shape=jax.ShapeDtypeStruct((n_rows * ROW_TILE, 128), F32),
        input_output_aliases={3: 0},
        compiler_params=_cparams(("arbitrary", "arbitrary")),
        name="moe_dispatch",
    )(dest_t, x, mods_i, jnp.zeros((n_rows * ROW_TILE, 128), F32))

    dh2 = w_gate_up.shape[-1]
    w_gate_up = w_gate_up.reshape(-1, d, dh2)
    b_gate_up = b_gate_up.reshape(-1, 1, dh2)
    w_down = w_down.reshape(-1, dh2 // 2, d)
    b_down = b_down.reshape(-1, 1, d)
    blk = lambda j, be, nu: (jnp.minimum(j, nu[0] - 1), 0)
    y_rows = pl.pallas_call(
        functools.partial(_expert_kernel, te),
        grid_spec=pltpu.PrefetchScalarGridSpec(
            num_scalar_prefetch=2,
            grid=(n_blocks,),
            in_specs=[
                pl.BlockSpec((te * ROW_TILE, 128), blk),
                pl.BlockSpec((None, d, dh2), lambda j, be, nu: (be[j], 0, 0)),
                pl.BlockSpec((None, 1, dh2), lambda j, be, nu: (be[j], 0, 0)),
                pl.BlockSpec((None, dh2 // 2, d), lambda j, be, nu: (be[j], 0, 0)),
                pl.BlockSpec((None, 1, d), lambda j, be, nu: (be[j], 0, 0)),
            ],
            out_specs=pl.BlockSpec((te * ROW_TILE, 128), lambda j, be, nu: (j, 0)),
            scratch_shapes=[
                pltpu.VMEM((d, dh2), BF16),
                pltpu.VMEM((dh2 // 2, d), BF16),
            ],
        ),
        out_shape=jax.ShapeDtypeStruct((n_rows * ROW_TILE, 128), F32),
        compiler_params=_cparams(("arbitrary",)),
        name="moe_experts",
    )(blk_expert, n_used, x_rows, w_gate_up, b_gate_up, w_down, b_down)

    gates_t = gates.T
    out = pl.pallas_call(
        functools.partial(_combine_kernel, alpha, tc),
        grid=(bsz, nct),
        in_specs=[
            pl.BlockSpec(memory_space=pl.ANY),
            pl.BlockSpec(memory_space=pl.ANY),
            pl.BlockSpec((None, tc, d), lambda b, i: (b, i, 0)),
            pl.BlockSpec((tc, TOP_K), lambda b, i: (b * nct + i, 0)),
            pl.BlockSpec((None, 6, d), lambda b, i: (b, 0, 0)),
            pl.BlockSpec((1, d), lambda b, i: (0, 0)),
            pl.BlockSpec((1, d), lambda b, i: (0, 0)),
        ],
        out_specs=pl.BlockSpec((None, tc, d), lambda b, i: (b, i, 0)),
        out_shape=jax.ShapeDtypeStruct((bsz, s, d), F32),
        scratch_shapes=[
            pltpu.SMEM((2, TOP_K * tc), jnp.int32),
            pltpu.VMEM((2, TOP_K, tc * ROW_TILE, 128), F32),
            pltpu.SemaphoreType.DMA((2,)),
            pltpu.SemaphoreType.DMA((2,)),
        ],
        compiler_params=_cparams(("arbitrary", "arbitrary")),
        name="moe_combine",
    )(dest_t, y_rows, x, gates_t, mods_i, ln_g.reshape(1, d), ln_b.reshape(1, d))
    return out


def kernel(x, c, ada_w, ada_b, ln_g, ln_b, mla_w_dqkv, mla_g_q, mla_g_kv, mla_w_uq, mla_w_ukv, mla_w_o,
           pool_w, pool_scale, gdn_w_in, gdn_w_conv, gdn_a_log, gdn_dt_bias, gdn_g_norm, gdn_w_o,
           moe_w_router, moe_b_router, moe_w_gate_up, moe_b_gate_up, moe_w_down, moe_b_down):
    depth = ada_w.shape[0]
    alpha = (2.0 * depth) ** 0.25
    mods = _ada_mods(c, ada_w, ada_b)
    for i in range(depth):
        kind, j = i % 3, i // 3
        m_i = mods[i]
        if kind == 0:
            x = _mla_layer(x, m_i, mla_w_dqkv[j], mla_g_q[j], mla_g_kv[j], mla_w_uq[j], mla_w_ukv[j], mla_w_o[j],
                           ln_g[i, 0], ln_b[i, 0], alpha)
        elif kind == 1:
            x = _pool_layer(x, m_i, pool_w[j], pool_scale[j], ln_g[i, 0], ln_b[i, 0], alpha)
        else:
            x = _gdn_layer(x, m_i, gdn_w_in[j], gdn_w_conv[j], gdn_a_log[j], gdn_dt_bias[j], gdn_g_norm[j],
                           gdn_w_o[j], ln_g[i, 0], ln_b[i, 0], alpha)
        x = _moe_layer(x, m_i, moe_w_router[i], moe_b_router[i], moe_w_gate_up, moe_b_gate_up,
                       moe_w_down, moe_b_down, ln_g[i, 1], ln_b[i, 1], alpha, layer=i)
    return x
```

```python
import functools
import math

import jax
import jax.numpy as jnp
from jax import lax
from jax.experimental import pallas as pl
from jax.experimental.pallas import tpu as pltpu

F32 = jnp.float32
BF16 = jnp.bfloat16

CHUNK = 64
LN_EPS = 1e-5
RMS_EPS = 1e-6

MLA_HEADS = 8
MLA_NOPE = 128
MLA_ROPE = 64
MLA_V = 128
MLA_Q_LORA = 384
MLA_KV_LORA = 256
ROPE_THETA = 10000.0
MLA_QK = 256

POOL_WINDOWS = (2, 4, 8, 16)
POOL_HALO = 16

GDN_QK_HEADS = 8
GDN_V_HEADS = 16
GDN_DK = 128
GDN_DV = 128
GDN_CONV = 4
GDN_Q_DIM = GDN_QK_HEADS * GDN_DK
GDN_V_DIM = GDN_V_HEADS * GDN_DV
GDN_CONV_DIM = 2 * GDN_Q_DIM + GDN_V_DIM
GDN_HALO = 8

N_EXPERTS = 32
TOP_K = 4
SWIGLU_LIMIT = 7.0
SWIGLU_ALPHA = 1.702

VMEM_LIMIT = 52 * 1024 * 1024
ROW_TILE = 8


def _cparams(sem):
    return pltpu.CompilerParams(dimension_semantics=sem, vmem_limit_bytes=VMEM_LIMIT)


def _bdot(a, b):
    return jnp.dot(a.astype(BF16), b.astype(BF16), preferred_element_type=F32)


def _bdot_nt(a, b):
    return lax.dot_general(a.astype(BF16), b.astype(BF16), (((1,), (1,)), ((), ())),
                           preferred_element_type=F32)


def _bdot_tn(a, b):
    return lax.dot_general(a.astype(BF16), b.astype(BF16), (((0,), (0,)), ((), ())),
                           preferred_element_type=F32)


def _split(a):
    hi = a.astype(BF16)
    lo = (a - hi.astype(F32)).astype(BF16)
    return hi, lo


def _dot3(a, b):
    ah, al = _split(a)
    bh, bl = _split(b)
    d = functools.partial(jnp.dot, preferred_element_type=F32)
    return d(ah, bh) + (d(ah, bl) + d(al, bh))


def _dot3_nt(a, b):
    ah, al = _split(a)
    bh, bl = _split(b)
    d = lambda x, y: lax.dot_general(x, y, (((1,), (1,)), ((), ())), preferred_element_type=F32)
    return d(ah, bh) + (d(ah, bl) + d(al, bh))


def _layer_norm(z, g, b):
    mu = jnp.mean(z, axis=-1, keepdims=True)
    zc = z - mu
    var = jnp.mean(zc * zc, axis=-1, keepdims=True)
    return zc * lax.rsqrt(var + LN_EPS) * g + b


def _sigmoid(x):
    return 1.0 / (1.0 + jnp.exp(-x))


def _silu(x):
    return x * _sigmoid(x)


def _ada_kernel(c_ref, w_ref, b_ref, o_ref):
    sc = _silu(c_ref[...])
    o_ref[...] = _dot3(sc, w_ref[...]) + b_ref[...]


def _ada_mods(c, ada_w, ada_b):
    depth, d, n = ada_w.shape
    bsz = c.shape[0]
    rows = 8
    tn = 2048
    c8 = jnp.zeros((rows, d), F32).at[:bsz].set(c)
    out = pl.pallas_call(
        _ada_kernel,
        grid=(depth, n // tn),
        in_specs=[
            pl.BlockSpec((rows, d), lambda l, j: (0, 0)),
            pl.BlockSpec((None, d, tn), lambda l, j: (l, 0, j)),
            pl.BlockSpec((None, 1, tn), lambda l, j: (l, 0, j)),
        ],
        out_specs=pl.BlockSpec((None, rows, tn), lambda l, j: (l, 0, j)),
        out_shape=jax.ShapeDtypeStruct((depth, rows, n), F32),
        compiler_params=_cparams(("arbitrary", "arbitrary")),
        name="ada_mods",
    )(c8, ada_w, ada_b.reshape(depth, 1, n))
    return out[:, :bsz].reshape(depth, bsz, 6, d)


def _route_io(bsz, s, d, tm):
    t = bsz * s
    nst = s // tm
    ne = N_EXPERTS
    in_specs = [pl.BlockSpec((ne, d), lambda b, i: (0, 0)), pl.BlockSpec((ne, 1), lambda b, i: (0, 0))]
    out_specs = [pl.BlockSpec((TOP_K, tm), lambda b, i: (0, b * nst + i))] * 3 + [
        pl.BlockSpec((ne, 128), lambda b, i: (0, 0))]
    out_shape = [jax.ShapeDtypeStruct((TOP_K, t), jnp.int32), jax.ShapeDtypeStruct((TOP_K, t), F32),
                 jax.ShapeDtypeStruct((TOP_K, t), jnp.int32), jax.ShapeDtypeStruct((ne, 128), F32)]
    return in_specs, out_specs, out_shape, [pltpu.VMEM((ne, 128), F32)]


def _post_kernel(alpha, gate_row, tm, y_ref, w_ref, x_ref, mod_ref, g_ref, b_ref, wr_ref, br_ref,
                 o_ref, idx_ref, gate_ref, rank_ref, cnt_ref, run):
    y = jnp.dot(y_ref[...], w_ref[...], preferred_element_type=F32)
    gate = mod_ref[gate_row:gate_row + 1, :]
    z = alpha * x_ref[...] + (1.0 + gate) * y
    x_new = _layer_norm(z, g_ref[...], b_ref[...])
    o_ref[...] = x_new
    _route_tile(tm, x_new, mod_ref, wr_ref, br_ref, idx_ref, gate_ref, rank_ref, cnt_ref, run)


def _post(y_pre, w_bf, x, mods_i, ln_g, ln_b, alpha, gate_row, w_router, b_router, tm=512):
    bsz, s, d = x.shape
    kd = y_pre.shape[-1]
    r_in, r_out, r_shape, r_scratch = _route_io(bsz, s, d, tm)
    outs = pl.pallas_call(
        functools.partial(_post_kernel, alpha, gate_row, tm),
        grid=(bsz, s // tm),
        in_specs=[
            pl.BlockSpec((None, tm, kd), lambda b, i: (b, i, 0)),
            pl.BlockSpec((kd, d), lambda b, i: (0, 0)),
            pl.BlockSpec((None, tm, d), lambda b, i: (b, i, 0)),
            pl.BlockSpec((None, 6, d), lambda b, i: (b, 0, 0)),
            pl.BlockSpec((1, d), lambda b, i: (0, 0)),
            pl.BlockSpec((1, d), lambda b, i: (0, 0)),
        ] + r_in,
        out_specs=[pl.BlockSpec((None, tm, d), lambda b, i: (b, i, 0))] + r_out,
        out_shape=[jax.ShapeDtypeStruct((bsz, s, d), F32)] + r_shape,
        scratch_shapes=r_scratch,
        compiler_params=_cparams(("arbitrary", "arbitrary")),
        name="post_proj_ln",
    )(y_pre, w_bf, x, mods_i, ln_g.reshape(1, d), ln_b.reshape(1, d), w_router.T, b_router.reshape(-1, 1))
    return outs[0], tuple(outs[1:])


def _mla_proj_kernel(qscale, x_ref, mod_ref, wd_ref, gq_ref, gkv_ref, wuq_ref, wukv_ref, cos_ref, sin_ref,
                     q_ref, k_ref, v_ref):
    shift = mod_ref[0:1, :]
    scale = mod_ref[1:2, :]
    h = x_ref[...] * (1.0 + scale) + shift
    down = _bdot(h, wd_ref[...])
    cq = down[:, :MLA_Q_LORA]
    ckv = down[:, MLA_Q_LORA:MLA_Q_LORA + MLA_KV_LORA]
    kra = down[:, 640:768]
    krb = down[:, 768:896]
    cos = cos_ref[...]
    sin = sin_ref[...]
    k_rope = (kra * cos + krb * sin).astype(BF16)

    cqn = cq * lax.rsqrt(jnp.mean(cq * cq, axis=-1, keepdims=True) + RMS_EPS) * gq_ref[...]
    ckvn = ckv * lax.rsqrt(jnp.mean(ckv * ckv, axis=-1, keepdims=True) + RMS_EPS) * gkv_ref[...]
    cqn = cqn.astype(BF16)
    ckvn = ckvn.astype(BF16)
    nh = MLA_HEADS
    for hh in range(nh):
        qn = jnp.dot(cqn, wuq_ref[:, hh * 128:(hh + 1) * 128], preferred_element_type=F32)
        qa = jnp.dot(cqn, wuq_ref[:, (nh + hh) * 128:(nh + hh + 1) * 128], preferred_element_type=F32)
        qb = jnp.dot(cqn, wuq_ref[:, (2 * nh + hh) * 128:(2 * nh + hh + 1) * 128], preferred_element_type=F32)
        q_ref[hh, :, 0:128] = (qn * qscale).astype(BF16)
        q_ref[hh, :, 128:256] = ((qa * cos + qb * sin) * qscale).astype(BF16)
        kv = jnp.dot(ckvn, wukv_ref[:, hh * 256:(hh + 1) * 256], preferred_element_type=F32)
        k_ref[hh, :, 0:128] = kv[:, :128].astype(BF16)
        k_ref[hh, :, 128:256] = k_rope
        v_ref[hh, :, :] = kv[:, 128:].astype(BF16)


MLA_HPS = 2


def _mla_attn_kernel(tq, tk, q_ref, k_ref, v_ref, o_ref):
    i = pl.program_id(2)
    nhs = MLA_HPS
    qs = [q_ref[h] for h in range(nhs)]

    def block(j, carry, masked):
        start = pl.multiple_of(j * tk, tk)
        ss = [lax.dot_general(qs[h], k_ref[h, pl.ds(start, tk), :], (((1,), (1,)), ((), ())),
                              preferred_element_type=F32) for h in range(nhs)]
        if masked:
            qc = (i * tq + lax.broadcasted_iota(jnp.int32, (tq, tk), 0)) // CHUNK
            kc = (j * tk + lax.broadcasted_iota(jnp.int32, (tq, tk), 1)) // CHUNK
            ss = [jnp.where(kc <= qc, s, -1e30) for s in ss]
        m_new = [jnp.maximum(carry[h][0], jnp.max(ss[h], axis=-1, keepdims=True)) for h in range(nhs)]
        ps = [jnp.exp(ss[h] - m_new[h]) for h in range(nhs)]
        corr = [jnp.exp(carry[h][0] - m_new[h]) for h in range(nhs)]
        l_new = [corr[h] * carry[h][1] + jnp.sum(ps[h], axis=-1, keepdims=True) for h in range(nhs)]
        pv = [jnp.dot(ps[h].astype(BF16), v_ref[h, pl.ds(start, tk), :], preferred_element_type=F32)
              for h in range(nhs)]
        return tuple((m_new[h], l_new[h], corr[h] * carry[h][2] + pv[h]) for h in range(nhs))

    init = tuple((jnp.full((tq, 1), -1e30, F32), jnp.zeros((tq, 1), F32), jnp.zeros((tq, MLA_V), F32))
                 for _ in range(nhs))
    ratio = tq // tk
    fin = lax.fori_loop(0, ratio * i, lambda j, c: block(j, c, False), init)
    for d in range(ratio):
        fin = block(ratio * i + d, fin, True)
    for h in range(nhs):
        o_ref[:, h * MLA_V:(h + 1) * MLA_V] = (fin[h][2] / fin[h][1]).astype(BF16)


def _mla_layer(x, mods_i, w_dqkv, g_q, g_kv, w_uq, w_ukv, w_o, ln_g, ln_b, alpha, w_router, b_router,
               tm=512, tq=512, tk=512):
    bsz, s, d = x.shape
    nh = MLA_HEADS
    half = MLA_ROPE // 2
    k1 = w_dqkv[:, 640:640 + half]
    k2 = w_dqkv[:, 640 + half:704]
    z64 = jnp.zeros((d, 64), F32)
    wd = jnp.concatenate([w_dqkv[:, :640], k1, k2, z64, k2, k1, z64], axis=1).astype(BF16)
    wq = w_uq.reshape(MLA_Q_LORA, nh, MLA_NOPE + MLA_ROPE)
    wq_n = wq[:, :, :MLA_NOPE].reshape(MLA_Q_LORA, nh * 128)
    x1 = wq[:, :, MLA_NOPE:MLA_NOPE + half]
    x2 = wq[:, :, MLA_NOPE + half:]
    zq = jnp.zeros((MLA_Q_LORA, nh, 64), F32)
    wq_a = jnp.concatenate([x1, x2, zq], axis=2).reshape(MLA_Q_LORA, nh * 128)
    wq_b = jnp.concatenate([x2, x1, zq], axis=2).reshape(MLA_Q_LORA, nh * 128)
    wuq = jnp.concatenate([wq_n, wq_a, wq_b], axis=1).astype(BF16)
    wukv = w_ukv.astype(BF16)
    pos = jnp.arange(s, dtype=F32)
    inv_freq = ROPE_THETA ** (-jnp.arange(0, MLA_ROPE, 2, dtype=F32) / MLA_ROPE)
    ang = pos[:, None] * inv_freq[None, :]
    cs, sn = jnp.cos(ang), jnp.sin(ang)
    zt = jnp.zeros((s, 64), F32)
    cos_t = jnp.concatenate([cs, cs, zt], axis=1)
    sin_t = jnp.concatenate([-sn, sn, zt], axis=1)
    qscale = (MLA_NOPE + MLA_ROPE) ** -0.5

    q, k, v = pl.pallas_call(
        functools.partial(_mla_proj_kernel, qscale),
        grid=(bsz, s // tm),
        in_specs=[
            pl.BlockSpec((None, tm, d), lambda b, i: (b, i, 0)),
            pl.BlockSpec((None, 6, d), lambda b, i: (b, 0, 0)),
            pl.BlockSpec(wd.shape, lambda b, i: (0, 0)),
            pl.BlockSpec((1, MLA_Q_LORA), lambda b, i: (0, 0)),
            pl.BlockSpec((1, MLA_KV_LORA), lambda b, i: (0, 0)),
            pl.BlockSpec(wuq.shape, lambda b, i: (0, 0)),
            pl.BlockSpec(wukv.shape, lambda b, i: (0, 0)),
            pl.BlockSpec((tm, 128), lambda b, i: (i, 0)),
            pl.BlockSpec((tm, 128), lambda b, i: (i, 0)),
        ],
        out_specs=[
            pl.BlockSpec((None, nh, tm, MLA_QK), lambda b, i: (b, 0, i, 0)),
            pl.BlockSpec((None, nh, tm, MLA_QK), lambda b, i: (b, 0, i, 0)),
            pl.BlockSpec((None, nh, tm, MLA_V), lambda b, i: (b, 0, i, 0)),
        ],
        out_shape=[
            jax.ShapeDtypeStruct((bsz, nh, s, MLA_QK), BF16),
            jax.ShapeDtypeStruct((bsz, nh, s, MLA_QK), BF16),
            jax.ShapeDtypeStruct((bsz, nh, s, MLA_V), BF16),
        ],
        compiler_params=_cparams(("arbitrary", "arbitrary")),
        name="mla_proj",
    )(x, mods_i, wd, g_q.reshape(1, -1), g_kv.reshape(1, -1), wuq, wukv, cos_t, sin_t)

    o = pl.pallas_call(
        functools.partial(_mla_attn_kernel, tq, tk),
        grid=(bsz, nh // MLA_HPS, s // tq),
        in_specs=[
            pl.BlockSpec((None, MLA_HPS, tq, MLA_QK), lambda b, h, i: (b, h, i, 0)),
            pl.BlockSpec((None, MLA_HPS, s, MLA_QK), lambda b, h, i: (b, h, 0, 0)),
            pl.BlockSpec((None, MLA_HPS, s, MLA_V), lambda b, h, i: (b, h, 0, 0)),
        ],
        out_specs=pl.BlockSpec((None, tq, MLA_HPS * MLA_V), lambda b, h, i: (b, i, h)),
        out_shape=jax.ShapeDtypeStruct((bsz, s, nh * MLA_V), BF16),
        compiler_params=_cparams(("arbitrary", "arbitrary", "arbitrary")),
        name="mla_attn",
    )(q, k, v)

    return _post(o, w_o.astype(BF16), x, mods_i, ln_g, ln_b, alpha, 2, w_router, b_router)


def _pool_kernel(alpha, tm, x_ref, halo_ref, mod_ref, w_ref, cs_ref, g_ref, b_ref, wr_ref, br_ref,
                 o_ref, idx_ref, gate_ref, rank_ref, cnt_ref, buf, run):
    i = pl.program_id(1)
    d = x_ref.shape[-1]
    gd = d // len(POOL_WINDOWS)
    shift = mod_ref[0:1, :]
    scale = mod_ref[1:2, :]
    gate = mod_ref[2:3, :]
    x = x_ref[...]
    h = x * (1.0 + scale) + shift
    hh = halo_ref[...] * (1.0 + scale) + shift
    hh = jnp.where(i > 0, hh, 0.0)
    buf[0:POOL_HALO, :] = hh
    buf[POOL_HALO:, :] = h
    t = i * tm + lax.broadcasted_iota(jnp.int32, (tm, 1), 0)
    ys = []
    for g, win in enumerate(POOL_WINDOWS):
        cols = slice(g * gd, (g + 1) * gd)
        acc = buf[POOL_HALO:POOL_HALO + tm, cols]
        for j in range(1, win):
            acc = acc + buf[POOL_HALO - j:POOL_HALO - j + tm, cols]
        count = jnp.minimum(t + 1, win).astype(F32)
        dg = acc / count - h[:, cols]
        ys.append(_bdot(dg, w_ref[g]))
    y = jnp.concatenate(ys, axis=-1) * cs_ref[...]
    z = alpha * x + (1.0 + gate) * y
    x_new = _layer_norm(z, g_ref[...], b_ref[...])
    o_ref[...] = x_new
    _route_tile(tm, x_new, mod_ref, wr_ref, br_ref, idx_ref, gate_ref, rank_ref, cnt_ref, run)


def _pool_layer(x, mods_i, w_pool, ch_scale, ln_g, ln_b, alpha, w_router, b_router, tm=512):
    bsz, s, d = x.shape
    gd = d // len(POOL_WINDOWS)
    hb = tm // POOL_HALO
    r_in, r_out, r_shape, r_scratch = _route_io(bsz, s, d, tm)
    outs = pl.pallas_call(
        functools.partial(_pool_kernel, alpha, tm),
        grid=(bsz, s // tm),
        in_specs=[
            pl.BlockSpec((None, tm, d), lambda b, i: (b, i, 0)),
            pl.BlockSpec((None, POOL_HALO, d), lambda b, i: (b, jnp.maximum(i * hb - 1, 0), 0)),
            pl.BlockSpec((None, 6, d), lambda b, i: (b, 0, 0)),
            pl.BlockSpec((len(POOL_WINDOWS), gd, gd), lambda b, i: (0, 0, 0)),
            pl.BlockSpec((1, d), lambda b, i: (0, 0)),
            pl.BlockSpec((1, d), lambda b, i: (0, 0)),
            pl.BlockSpec((1, d), lambda b, i: (0, 0)),
        ] + r_in,
        out_specs=[pl.BlockSpec((None, tm, d), lambda b, i: (b, i, 0))] + r_out,
        out_shape=[jax.ShapeDtypeStruct((bsz, s, d), F32)] + r_shape,
        scratch_shapes=[pltpu.VMEM((tm + POOL_HALO, d), F32)] + r_scratch,
        compiler_params=_cparams(("arbitrary", "arbitrary")),
        name="pool_layer",
    )(x, x, mods_i, w_pool.astype(BF16), ch_scale.reshape(1, d), ln_g.reshape(1, d), ln_b.reshape(1, d),
      w_router.T, b_router.reshape(-1, 1))
    return outs[0], tuple(outs[1:])


def _gdn_proj_kernel(tm, x_ref, halo_ref, mod_ref, wqkv_ref, wz_ref, wba_ref, wc_ref, alog_ref, dtb_ref,
                     q_ref, k_ref, v_ref, z_ref, beta_ref, gc_ref, hbuf, pbuf):
    i = pl.program_id(1)
    shift = mod_ref[0:1, :]
    scale = mod_ref[1:2, :]
    h = x_ref[...] * (1.0 + scale) + shift
    hh = halo_ref[...] * (1.0 + scale) + shift
    hh = jnp.where(i > 0, hh, 0.0)
    hbuf[0:GDN_HALO, :] = hh
    hbuf[GDN_HALO:, :] = h
    h_ext = hbuf[...].astype(BF16)

    z_ref[...] = jnp.dot(h.astype(BF16), wz_ref[...], preferred_element_type=F32).astype(BF16)

    ba = _dot3(h, wba_ref[...])
    nv = GDN_V_HEADS
    beta_ref[...] = _sigmoid(ba[:, :nv])
    a = ba[:, nv:] + dtb_ref[...]
    softplus = jnp.maximum(a, 0.0) + jnp.log(1.0 + jnp.exp(-jnp.abs(a)))
    g = -jnp.exp(alog_ref[...]) * softplus
    r = lax.broadcasted_iota(jnp.int32, (tm, tm), 0)
    c = lax.broadcasted_iota(jnp.int32, (tm, tm), 1)
    tri = jnp.where((c <= r) & (c // CHUNK == r // CHUNK), 1.0, 0.0).astype(BF16)
    g_hi = g.astype(BF16)
    g_lo = (g - g_hi.astype(F32)).astype(BF16)
    g_lo2 = (g - g_hi.astype(F32) - g_lo.astype(F32)).astype(BF16)
    d = functools.partial(jnp.dot, preferred_element_type=F32)
    gc_ref[...] = d(tri, g_hi) + (d(tri, g_lo) + d(tri, g_lo2))

    cw = 512
    for cb in range(GDN_CONV_DIM // cw):
        cols = slice(cb * cw, (cb + 1) * cw)
        pbuf[...] = jnp.dot(h_ext, wqkv_ref[:, cols], preferred_element_type=F32)
        acc = pbuf[GDN_HALO:GDN_HALO + tm, :] * wc_ref[GDN_CONV - 1:GDN_CONV, cols]
        for j in range(GDN_CONV - 1):
            off = GDN_HALO - (GDN_CONV - 1) + j
            acc = acc + pbuf[off:off + tm, :] * wc_ref[j:j + 1, cols]
        y = _silu(acc)
        if cb * cw < 2 * GDN_Q_DIM:
            outs = []
            for hd in range(cw // GDN_DK):
                yh = y[:, hd * GDN_DK:(hd + 1) * GDN_DK]
                outs.append(yh * lax.rsqrt(jnp.sum(yh * yh, axis=-1, keepdims=True) + RMS_EPS))
            y = jnp.concatenate(outs, axis=-1)
            if cb * cw < GDN_Q_DIM:
                q_ref[:, cols] = (y * (GDN_DK ** -0.5)).astype(BF16)
            else:
                k_ref[:, cb * cw - GDN_Q_DIM:(cb + 1) * cw - GDN_Q_DIM] = y.astype(BF16)
        else:
            v_ref[:, cb * cw - 2 * GDN_Q_DIM:(cb + 1) * cw - 2 * GDN_Q_DIM] = y.astype(BF16)


def _col(x, hidx):
    lane = lax.broadcasted_iota(jnp.int32, x.shape, 1)
    return jnp.sum(jnp.where(lane == hidx, x, 0.0), axis=-1, keepdims=True)


GDN_HPS = 4


def _tile_rows(x, n):
    return jnp.concatenate([x] * n, axis=0)


def _gdn_chunk_kernel(tb, q_ref, k_ref, v_ref, z_ref, beta_ref, gc_ref, gr_ref, gn_ref, o_ref, s01, s23):
    hg = pl.program_id(1)
    n = pl.program_id(2)
    cs = CHUNK
    hp = GDN_HPS
    dv = GDN_DV
    wp = hp * cs
    wv = hp * dv

    @pl.when(n == 0)
    def _():
        s01[...] = jnp.zeros_like(s01)
        s23[...] = jnp.zeros_like(s23)

    bcols = [_col(beta_ref[...], hp * hg + i) for i in range(hp)]
    gcols = [_col(gc_ref[...], hp * hg + i) for i in range(hp)]
    beta_b = jnp.concatenate([jnp.broadcast_to(cc, (tb, dv)) for cc in bcols], axis=1)
    g_b = jnp.concatenate([jnp.broadcast_to(cc, (tb, dv)) for cc in gcols], axis=1)
    lane_p = lax.broadcasted_iota(jnp.int32, (tb, wp), 1)
    g_p = jnp.broadcast_to(gcols[hp - 1], (tb, wp))
    for i in reversed(range(hp - 1)):
        g_p = jnp.where(lane_p < (i + 1) * cs, gcols[i], g_p)

    r = lax.broadcasted_iota(jnp.int32, (cs, wp), 0)
    lc = lax.broadcasted_iota(jnp.int32, (cs, wp), 1) & (cs - 1)
    tril_incl = r >= lc
    tril_strict = r > lc
    eye_p = jnp.where(r == lc, 1.0, 0.0).astype(F32)
    rr = lax.broadcasted_iota(jnp.int32, (wp, wp), 0)
    ll = lax.broadcasted_iota(jnp.int32, (wp, wp), 1)
    bd_mask = (rr // cs) == (ll // cs)
    pair_mask = (rr // dv) == (ll // dv)
    rv = lax.broadcasted_iota(jnp.int32, (wp, wv), 0)
    lv = lax.broadcasted_iota(jnp.int32, (wp, wv), 1)
    bdv_mask = (rv // cs) == (lv // dv)

    def bd(y):
        return jnp.where(bd_mask, _tile_rows(y, hp), 0.0).astype(BF16)

    def bdv(y):
        return jnp.where(bdv_mask, _tile_rows(y, hp), 0.0).astype(BF16)

    def mm(a, b):
        return jnp.dot(a.astype(BF16), b, preferred_element_type=F32)

    def mm_nt(a, b):
        return lax.dot_general(a.astype(BF16), b, (((1,), (1,)), ((), ())), preferred_element_type=F32)

    states = [s01[...], s23[...]]
    gn = jnp.concatenate([gn_ref[...]] * hp, axis=1)
    nch = tb // cs
    pre = []
    for c in range(nch):
        rows = slice(c * cs, (c + 1) * cs)
        q2 = q_ref[rows, :].astype(F32)
        k2 = k_ref[rows, :].astype(F32)
        q4 = jnp.concatenate([q2[:, :dv], q2[:, :dv], q2[:, dv:], q2[:, dv:]], axis=1)
        k4 = jnp.concatenate([k2[:, :dv], k2[:, :dv], k2[:, dv:], k2[:, dv:]], axis=1)
        v4 = v_ref[rows, :].astype(F32)
        bb = beta_b[rows, :]
        gb = g_b[rows, :]
        gp = g_p[rows, :]
        grow = gr_ref[c:c + 1, :]
        decay = jnp.exp(jnp.where(tril_incl, gp - grow, -jnp.inf))
        eg = jnp.exp(gb)
        glast = gb[cs - 1:cs, :]
        kb = k4 * bb
        vb = v4 * bb
        kbd = bdv(k4)
        kk = mm_nt(kb, kbd)
        xm = -jnp.where(tril_strict, kk * decay, 0.0)
        a_intra = mm_nt(q4, kbd) * decay
        pre.append(dict(vbd=bdv(vb), kgd=bdv(kb * eg), a=a_intra, qg=q4 * eg, kd=k4 * jnp.exp(glast - gb),
                        gl=jnp.exp(glast), xm=xm))
    tinvs = [eye_p + pc["xm"] for pc in pre]
    pws = [pc["xm"] for pc in pre]
    bdps = [bd(pw) for pw in pws]
    for _ in range(5):
        pws = [mm(pw, bdp) for pw, bdp in zip(pws, bdps)]
        bdps = [bd(pw) for pw in pws]
        tinvs = [tinv + mm(tinv, bdp) for tinv, bdp in zip(tinvs, bdps)]
    us = [mm(tinv, pc["vbd"]) for tinv, pc in zip(tinvs, pre)]
    ws = [mm(tinv, pc["kgd"]) for tinv, pc in zip(tinvs, pre)]
    for c in range(nch):
        rows = slice(c * cs, (c + 1) * cs)
        u, w, a_intra, qg, kd, gl = us[c], ws[c], pre[c]["a"], pre[c]["qg"], pre[c]["kd"], pre[c]["gl"]
        vn, oq = [], []
        for p in range(2):
            sl = slice(p * 2 * dv, (p + 1) * 2 * dv)
            sb = states[p].astype(BF16)
            vn.append(u[:, sl] - mm(w[:, sl], sb))
            oq.append(mm(qg[:, sl], sb))
        v_new = jnp.concatenate(vn, axis=1)
        o = jnp.concatenate(oq, axis=1) + mm(a_intra, bdv(v_new))
        for p in range(2):
            sl = slice(p * 2 * dv, (p + 1) * 2 * dv)
            upd = lax.dot_general(kd[:, sl].astype(BF16), v_new[:, sl].astype(BF16), (((0,), (0,)), ((), ())),
                                  preferred_element_type=F32)
            states[p] = states[p] * gl[:, sl] + jnp.where(pair_mask, upd, 0.0)
        outs = []
        for i in range(hp):
            oi = o[:, i * dv:(i + 1) * dv]
            outs.append(oi * lax.rsqrt(jnp.mean(oi * oi, axis=-1, keepdims=True) + RMS_EPS))
        on = jnp.concatenate(outs, axis=1) * gn
        zz = z_ref[rows, :].astype(F32)
        o_ref[rows, :] = (on * _silu(zz)).astype(BF16)
    s01[...] = states[0]
    s23[...] = states[1]


def _gdn_layer(x, mods_i, w_in, w_conv, a_log, dt_bias, g_norm, w_o, ln_g, ln_b, alpha, w_router, b_router,
               tm=256, tb=512):
    bsz, s, d = x.shape
    nv = GDN_V_HEADS
    wqkv = w_in[:, :GDN_CONV_DIM].astype(BF16)
    wz = w_in[:, GDN_CONV_DIM:GDN_CONV_DIM + GDN_V_DIM].astype(BF16)
    wba = w_in[:, GDN_CONV_DIM + GDN_V_DIM:]
    hb = tm // GDN_HALO
    q, k, v, z, beta, gc = pl.pallas_call(
        functools.partial(_gdn_proj_kernel, tm),
        grid=(bsz, s // tm),
        in_specs=[
            pl.BlockSpec((None, tm, d), lambda b, i: (b, i, 0)),
            pl.BlockSpec((None, GDN_HALO, d), lambda b, i: (b, jnp.maximum(i * hb - 1, 0), 0)),
            pl.BlockSpec((None, 6, d), lambda b, i: (b, 0, 0)),
            pl.BlockSpec(wqkv.shape, lambda b, i: (0, 0)),
            pl.BlockSpec(wz.shape, lambda b, i: (0, 0)),
            pl.BlockSpec(wba.shape, lambda b, i: (0, 0)),
            pl.BlockSpec((GDN_CONV, GDN_CONV_DIM), lambda b, i: (0, 0)),
            pl.BlockSpec((1, nv), lambda b, i: (0, 0)),
            pl.BlockSpec((1, nv), lambda b, i: (0, 0)),
        ],
        out_specs=[
            pl.BlockSpec((None, tm, GDN_Q_DIM), lambda b, i: (b, i, 0)),
            pl.BlockSpec((None, tm, GDN_Q_DIM), lambda b, i: (b, i, 0)),
            pl.BlockSpec((None, tm, GDN_V_DIM), lambda b, i: (b, i, 0)),
            pl.BlockSpec((None, tm, GDN_V_DIM), lambda b, i: (b, i, 0)),
            pl.BlockSpec((None, tm, nv), lambda b, i: (b, i, 0)),
            pl.BlockSpec((None, tm, nv), lambda b, i: (b, i, 0)),
        ],
        out_shape=[
            jax.ShapeDtypeStruct((bsz, s, GDN_Q_DIM), BF16),
            jax.ShapeDtypeStruct((bsz, s, GDN_Q_DIM), BF16),
            jax.ShapeDtypeStruct((bsz, s, GDN_V_DIM), BF16),
            jax.ShapeDtypeStruct((bsz, s, GDN_V_DIM), BF16),
            jax.ShapeDtypeStruct((bsz, s, nv), F32),
            jax.ShapeDtypeStruct((bsz, s, nv), F32),
        ],
        scratch_shapes=[pltpu.VMEM((tm + GDN_HALO, d), F32), pltpu.VMEM((tm + GDN_HALO, 512), F32)],
        compiler_params=_cparams(("arbitrary", "arbitrary")),
        name="gdn_proj",
    )(x, x, mods_i, wqkv, wz, wba, w_conv, a_log.reshape(1, nv), dt_bias.reshape(1, nv))

    hp = GDN_HPS
    nc = s // CHUNK
    gr = gc.reshape(bsz, nc, CHUNK, nv // hp, hp).transpose(0, 3, 1, 4, 2).reshape(bsz, nv // hp, nc, hp * CHUNK)
    rep = GDN_V_HEADS // GDN_QK_HEADS
    cpb = tb // CHUNK
    qw = hp // rep * GDN_DK
    o = pl.pallas_call(
        functools.partial(_gdn_chunk_kernel, tb),
        grid=(bsz, nv // hp, s // tb),
        in_specs=[
            pl.BlockSpec((None, tb, qw), lambda b, h, n: (b, n, h)),
            pl.BlockSpec((None, tb, qw), lambda b, h, n: (b, n, h)),
            pl.BlockSpec((None, tb, hp * GDN_DV), lambda b, h, n: (b, n, h)),
            pl.BlockSpec((None, tb, hp * GDN_DV), lambda b, h, n: (b, n, h)),
            pl.BlockSpec((None, tb, nv), lambda b, h, n: (b, n, 0)),
            pl.BlockSpec((None, tb, nv), lambda b, h, n: (b, n, 0)),
            pl.BlockSpec((None, None, cpb, hp * CHUNK), lambda b, h, n: (b, h, n, 0)),
            pl.BlockSpec((1, GDN_DV), lambda b, h, n: (0, 0)),
        ],
        out_specs=pl.BlockSpec((None, tb, hp * GDN_DV), lambda b, h, n: (b, n, h)),
        out_shape=jax.ShapeDtypeStruct((bsz, s, GDN_V_DIM), BF16),
        scratch_shapes=[pltpu.VMEM((2 * GDN_DK, 2 * GDN_DV), F32), pltpu.VMEM((2 * GDN_DK, 2 * GDN_DV), F32)],
        compiler_params=_cparams(("arbitrary", "arbitrary", "arbitrary")),
        name="gdn_chunk",
    )(q, k, v, z, beta, gc, gr, g_norm.reshape(1, GDN_DV))

    return _post(o, w_o.astype(BF16), x, mods_i, ln_g, ln_b, alpha, 2, w_router, b_router)


def _route_tile(tm, x_new, mod_ref, wr_ref, br_ref, idx_ref, gate_ref, rank_ref, cnt_ref, run):
    step = pl.program_id(0) * pl.num_programs(1) + pl.program_id(1)

    @pl.when(step == 0)
    def _():
        run[...] = jnp.zeros_like(run)

    shift = mod_ref[3:4, :]
    scale = mod_ref[4:5, :]
    h = x_new * (1.0 + scale) + shift
    logits = _dot3_nt(wr_ref[...], h) + br_ref[...]
    eid = lax.broadcasted_iota(jnp.int32, (N_EXPERTS, tm), 0)
    vals, idxs, hots = [], [], []
    cur = logits
    for _ in range(TOP_K):
        m = jnp.max(cur, axis=0, keepdims=True)
        sel = jnp.min(jnp.where(cur == m, eid, N_EXPERTS), axis=0, keepdims=True)
        hot = eid == sel
        vals.append(m)
        idxs.append(sel)
        hots.append(hot)
        cur = jnp.where(hot, -jnp.inf, cur)
    es = [jnp.exp(v - vals[0]) for v in vals]
    den = es[0] + es[1] + es[2] + es[3]
    cnt = jnp.zeros((N_EXPERTS, tm), F32)
    for hot in hots:
        cnt = cnt + jnp.where(hot, 1.0, 0.0)
    r = lax.broadcasted_iota(jnp.int32, (tm, tm), 0)
    c = lax.broadcasted_iota(jnp.int32, (tm, tm), 1)
    before = jnp.where(r < c, 1.0, 0.0).astype(BF16)
    prefix = jnp.dot(cnt.astype(BF16), before, preferred_element_type=F32) + run[:, 0:1]
    for kk in range(TOP_K):
        idx_ref[kk:kk + 1, :] = idxs[kk]
        gate_ref[kk:kk + 1, :] = es[kk] / den
        rank_ref[kk:kk + 1, :] = jnp.sum(jnp.where(hots[kk], prefix, 0.0), axis=0, keepdims=True).astype(jnp.int32)
    run[...] = run[...] + jnp.sum(cnt, axis=1, keepdims=True)
    cnt_ref[...] = run[...]


def _dispatch_kernel(tm, dest_hbm, x_ref, mod_ref, zeros_hbm, rows_hbm, d_smem, hbuf, d_sem, s_sem):
    del zeros_hbm
    step = pl.program_id(0) * pl.num_programs(1) + pl.program_id(1)
    nsteps = pl.num_programs(0) * pl.num_programs(1)
    slot = step % 2
    nxt = 1 - slot

    def d_copy(blk, sl):
        return pltpu.make_async_copy(dest_hbm.at[blk], d_smem.at[sl], d_sem.at[sl])

    def wait_scatter(sl):
        for _ in range(TOP_K):
            pltpu.make_async_copy(hbuf.at[sl], rows_hbm.at[pl.ds(0, tm * ROW_TILE), :], s_sem.at[sl]).wait()

    @pl.when(step == 0)
    def _():
        d_copy(0, 0).start()

    @pl.when(step >= 2)
    def _():
        wait_scatter(slot)

    shift = mod_ref[3:4, :]
    scale = mod_ref[4:5, :]
    h = x_ref[...] * (1.0 + scale) + shift
    for sc in range(ROW_TILE):
        hbuf[slot, pl.ds(sc, tm, stride=ROW_TILE), :] = h[:, sc * 128:(sc + 1) * 128]

    d_copy(step, slot).wait()

    @pl.when(step + 1 < nsteps)
    def _():
        d_copy(step + 1, nxt).start()

    def body(r, carry):
        src = hbuf.at[slot, pl.ds(pl.multiple_of(r * ROW_TILE, ROW_TILE), ROW_TILE), :]
        for kk in range(TOP_K):
            dst = d_smem[slot, kk * tm + r]
            out = rows_hbm.at[pl.ds(pl.multiple_of(dst * ROW_TILE, ROW_TILE), ROW_TILE), :]
            pltpu.make_async_copy(src, out, s_sem.at[slot]).start(priority=kk % 2)
        return carry
    lax.fori_loop(0, tm, body, 0, unroll=4)

    @pl.when(step == nsteps - 1)
    def _():
        wait_scatter(slot)

        @pl.when(nsteps > 1)
        def _():
            wait_scatter(nxt)


def _expert_kernel(tm, be_ref, nx_ref, sl_ref, nu_ref, x_ref, wgu_hbm, bgu_ref, wd_hbm, bd_ref, y_ref,
                   wgu_f32, wd_f32, wgu_bf, wd_bf, w_sem):
    j = pl.program_id(0)
    n_used = nu_ref[0]
    changed = jnp.logical_or(j == 0, be_ref[j] != be_ref[jnp.maximum(j - 1, 0)])
    slot = sl_ref[j]

    def fetch(row, s):
        return (pltpu.make_async_copy(wgu_hbm.at[row], wgu_f32.at[s], w_sem.at[s, 0]),
                pltpu.make_async_copy(wd_hbm.at[row], wd_f32.at[s], w_sem.at[s, 1]))

    @pl.when(j == 0)
    def _():
        for cp in fetch(be_ref[0], 0):
            cp.start()

    @pl.when(jnp.logical_and(changed, j < n_used))
    def _():
        for cp in fetch(be_ref[j], slot):
            cp.wait()

        @pl.when(nx_ref[j] >= 0)
        def _():
            for cp in fetch(nx_ref[j], 1 - slot):
                cp.start()

        wgu_bf[...] = wgu_f32[slot].astype(BF16)
        wd_bf[...] = wd_f32[slot].astype(BF16)

    @pl.when(j >= n_used)
    def _():
        y_ref[...] = jnp.zeros_like(y_ref)

    @pl.when(j < n_used)
    def _():
        nsc = ROW_TILE
        xb = jnp.concatenate([x_ref[pl.ds(sc, tm, stride=nsc), :] for sc in range(nsc)], axis=-1).astype(BF16)
        gu = jnp.dot(xb, wgu_bf[...], preferred_element_type=F32) + bgu_ref[...]
        dh = gu.shape[-1] // 2
        gate = jnp.minimum(gu[:, :dh], SWIGLU_LIMIT)
        up = jnp.clip(gu[:, dh:], -SWIGLU_LIMIT, SWIGLU_LIMIT)
        act = gate * _sigmoid(SWIGLU_ALPHA * gate) * (up + 1.0)
        y = jnp.dot(act.astype(BF16), wd_bf[...], preferred_element_type=F32) + bd_ref[...]
        for sc in range(nsc):
            y_ref[pl.ds(sc, tm, stride=nsc), :] = y[:, sc * 128:(sc + 1) * 128]


def _combine_kernel(alpha, tm, dest_hbm, y_hbm, x_ref, gate_ref, mod_ref, g_ref, b_ref, o_ref,
                    d_smem, ybuf, d_sem, y_sem):
    step = pl.program_id(0) * pl.num_programs(1) + pl.program_id(1)
    nsteps = pl.num_programs(0) * pl.num_programs(1)
    slot = step % 2
    nxt = 1 - slot

    def d_copy(blk, sl):
        return pltpu.make_async_copy(dest_hbm.at[blk], d_smem.at[sl], d_sem.at[sl])

    def issue_rows(sl):
        def body(r, carry):
            for kk in range(TOP_K):
                dst = d_smem[sl, kk * tm + r]
                src = y_hbm.at[pl.ds(pl.multiple_of(dst * ROW_TILE, ROW_TILE), ROW_TILE), :]
                dbuf = ybuf.at[sl, kk, pl.ds(pl.multiple_of(r * ROW_TILE, ROW_TILE), ROW_TILE), :]
                pltpu.make_async_copy(src, dbuf, y_sem.at[sl]).start(priority=kk % 2)
            return carry
        lax.fori_loop(0, tm, body, 0, unroll=4)

    @pl.when(step == 0)
    def _():
        d_copy(0, 0).start()
        d_copy(0, 0).wait()
        issue_rows(0)

        @pl.when(nsteps > 1)
        def _():
            d_copy(1, 1).start()

    @pl.when(step + 1 < nsteps)
    def _():
        d_copy(step + 1, nxt).wait()
        issue_rows(nxt)

    @pl.when(step + 2 < nsteps)
    def _():
        d_copy(step + 2, slot).start()

    for kk in range(TOP_K):
        pltpu.make_async_copy(y_hbm.at[pl.ds(0, tm * ROW_TILE), :], ybuf.at[slot, kk], y_sem.at[slot]).wait()
    g = gate_ref[...]
    parts = []
    for sc in range(ROW_TILE):
        acc = ybuf[slot, 0, pl.ds(sc, tm, stride=ROW_TILE), :] * g[:, 0:1]
        for kk in range(1, TOP_K):
            acc = acc + ybuf[slot, kk, pl.ds(sc, tm, stride=ROW_TILE), :] * g[:, kk:kk + 1]
        parts.append(acc)
    y = jnp.concatenate(parts, axis=-1)
    gate_f = mod_ref[5:6, :]
    z = alpha * x_ref[...] + (1.0 + gate_f) * y
    o_ref[...] = _layer_norm(z, g_ref[...], b_ref[...])


MOE_TE = 256


def _moe_rows_buffer(t):
    return jnp.zeros(((t * TOP_K + N_EXPERTS * MOE_TE) * ROW_TILE, 128), F32)


def _moe_layer(x, mods_i, routing, rows_buf, w_gate_up, b_gate_up, w_down, b_down, ln_g, ln_b, alpha,
               layer=0, tc=256):
    bsz, s, d = x.shape
    assert d == ROW_TILE * 128, "token rows are stored as one (8,128) f32 tile"
    t = bsz * s
    ne = N_EXPERTS
    te = MOE_TE
    idx, gates, rank, counts = routing

    cnt = counts[:, 0].astype(jnp.int32)
    padded = (cnt + te - 1) // te * te
    pend = jnp.cumsum(padded)
    pstart = pend - padded
    n_rows = t * TOP_K + ne * te
    n_blocks = n_rows // te
    eids = jnp.arange(ne, dtype=jnp.int32)
    dest = jnp.sum(jnp.where(idx[None] == eids[:, None, None], pstart[:, None, None], 0), axis=0) + rank
    n_used = (pend[-1] // te).astype(jnp.int32).reshape(1)
    blk_start = jnp.minimum(jnp.arange(n_blocks, dtype=jnp.int32) * te, pend[-1] - te)
    blk_expert = jnp.minimum(jnp.sum((blk_start[:, None] >= pend[None, :]).astype(jnp.int32), axis=1), ne - 1)
    present = padded > 0
    nxt = lax.cummin(jnp.where(present, eids, ne)[::-1])[::-1]
    next_present = jnp.concatenate([nxt[1:], jnp.full((1,), ne, jnp.int32)])
    run_id = jnp.cumsum(present.astype(jnp.int32)) - 1
    onehot = blk_expert[:, None] == eids[None, :]
    blk_next = jnp.sum(jnp.where(onehot, next_present[None, :], 0), axis=1)
    blk_slot = jnp.sum(jnp.where(onehot, run_id[None, :], 0), axis=1) % 2
    blk_next = jnp.where(blk_next < ne, blk_next + layer * ne, -1)
    blk_expert = blk_expert + layer * ne

    nct = s // tc
    dest_t = dest.reshape(TOP_K, t // tc, tc).transpose(1, 0, 2).reshape(t // tc, TOP_K * tc)
    x_rows = pl.pallas_call(
        functools.partial(_dispatch_kernel, tc),
        grid=(bsz, nct),
        in_specs=[
            pl.BlockSpec(memory_space=pl.ANY),
            pl.BlockSpec((None, tc, d), lambda b, i: (b, i, 0)),
            pl.BlockSpec((None, 6, d), lambda b, i: (b, 0, 0)),
            pl.BlockSpec(memory_space=pl.ANY),
        ],
        out_specs=pl.BlockSpec(memory_space=pl.ANY),
        scratch_shapes=[
            pltpu.SMEM((2, TOP_K * tc), jnp.int32),
            pltpu.VMEM((2, tc * ROW_TILE, 128), F32),
            pltpu.SemaphoreType.DMA((2,)),
            pltpu.SemaphoreType.DMA((2,)),
        ],
        out_shape=jax.ShapeDtypeStruct((n_rows * ROW_TILE, 128), F32),
        input_output_aliases={3: 0},
        compiler_params=_cparams(("arbitrary", "arbitrary")),
        name="moe_dispatch",
    )(dest_t, x, mods_i, rows_buf)

    dh2 = w_gate_up.shape[-1]
    w_gate_up = w_gate_up.reshape(-1, d, dh2)
    b_gate_up = b_gate_up.reshape(-1, 1, dh2)
    w_down = w_down.reshape(-1, dh2 // 2, d)
    b_down = b_down.reshape(-1, 1, d)
    y_rows = pl.pallas_call(
        functools.partial(_expert_kernel, te),
        grid_spec=pltpu.PrefetchScalarGridSpec(
            num_scalar_prefetch=4,
            grid=(n_blocks,),
            in_specs=[
                pl.BlockSpec((te * ROW_TILE, 128), lambda j, be, nx, sl, nu: (jnp.minimum(j, nu[0] - 1), 0)),
                pl.BlockSpec(memory_space=pl.ANY),
                pl.BlockSpec((None, 1, dh2), lambda j, be, nx, sl, nu: (be[j], 0, 0)),
                pl.BlockSpec(memory_space=pl.ANY),
                pl.BlockSpec((None, 1, d), lambda j, be, nx, sl, nu: (be[j], 0, 0)),
            ],
            out_specs=pl.BlockSpec((te * ROW_TILE, 128), lambda j, be, nx, sl, nu: (j, 0)),
            scratch_shapes=[
                pltpu.VMEM((2, d, dh2), F32),
                pltpu.VMEM((2, dh2 // 2, d), F32),
                pltpu.VMEM((d, dh2), BF16),
                pltpu.VMEM((dh2 // 2, d), BF16),
                pltpu.SemaphoreType.DMA((2, 2)),
            ],
        ),
        out_shape=jax.ShapeDtypeStruct((n_rows * ROW_TILE, 128), F32),
        compiler_params=_cparams(("arbitrary",)),
        name="moe_experts",
    )(blk_expert, blk_next, blk_slot, n_used, x_rows, w_gate_up, b_gate_up, w_down, b_down)

    gates_t = gates.T
    out = pl.pallas_call(
        functools.partial(_combine_kernel, alpha, tc),
        grid=(bsz, nct),
        in_specs=[
            pl.BlockSpec(memory_space=pl.ANY),
            pl.BlockSpec(memory_space=pl.ANY),
            pl.BlockSpec((None, tc, d), lambda b, i: (b, i, 0)),
            pl.BlockSpec((tc, TOP_K), lambda b, i: (b * nct + i, 0)),
            pl.BlockSpec((None, 6, d), lambda b, i: (b, 0, 0)),
            pl.BlockSpec((1, d), lambda b, i: (0, 0)),
            pl.BlockSpec((1, d), lambda b, i: (0, 0)),
        ],
        out_specs=pl.BlockSpec((None, tc, d), lambda b, i: (b, i, 0)),
        out_shape=jax.ShapeDtypeStruct((bsz, s, d), F32),
        scratch_shapes=[
            pltpu.SMEM((2, TOP_K * tc), jnp.int32),
            pltpu.VMEM((2, TOP_K, tc * ROW_TILE, 128), F32),
            pltpu.SemaphoreType.DMA((2,)),
            pltpu.SemaphoreType.DMA((2,)),
        ],
        compiler_params=_cparams(("arbitrary", "arbitrary")),
        name="moe_combine",
    )(dest_t, y_rows, x, gates_t, mods_i, ln_g.reshape(1, d), ln_b.reshape(1, d))
    return out, x_rows


def kernel(x, c, ada_w, ada_b, ln_g, ln_b, mla_w_dqkv, mla_g_q, mla_g_kv, mla_w_uq, mla_w_ukv, mla_w_o,
           pool_w, pool_scale, gdn_w_in, gdn_w_conv, gdn_a_log, gdn_dt_bias, gdn_g_norm, gdn_w_o,
           moe_w_router, moe_b_router, moe_w_gate_up, moe_b_gate_up, moe_w_down, moe_b_down):
    depth = ada_w.shape[0]
    alpha = (2.0 * depth) ** 0.25
    mods = _ada_mods(c, ada_w, ada_b)
    rows_buf = _moe_rows_buffer(x.shape[0] * x.shape[1])
    for i in range(depth):
        kind, j = i % 3, i // 3
        m_i = mods[i]
        wr, br = moe_w_router[i], moe_b_router[i]
        if kind == 0:
            x, routing = _mla_layer(x, m_i, mla_w_dqkv[j], mla_g_q[j], mla_g_kv[j], mla_w_uq[j], mla_w_ukv[j],
                                    mla_w_o[j], ln_g[i, 0], ln_b[i, 0], alpha, wr, br)
        elif kind == 1:
            x, routing = _pool_layer(x, m_i, pool_w[j], pool_scale[j], ln_g[i, 0], ln_b[i, 0], alpha, wr, br)
        else:
            x, routing = _gdn_layer(x, m_i, gdn_w_in[j], gdn_w_conv[j], gdn_a_log[j], gdn_dt_bias[j],
                                    gdn_g_norm[j], gdn_w_o[j], ln_g[i, 0], ln_b[i, 0], alpha, wr, br)
        x, rows_buf = _moe_layer(x, m_i, routing, rows_buf, moe_w_gate_up, moe_b_gate_up,
                                 moe_w_down, moe_b_down, ln_g[i, 1], ln_b[i, 1], alpha, layer=i)
    return x
```

```python
import functools
import math

import jax
import jax.numpy as jnp
from jax import lax
from jax.experimental import pallas as pl
from jax.experimental.pallas import tpu as pltpu

F32 = jnp.float32
BF16 = jnp.bfloat16

CHUNK = 64
LN_EPS = 1e-5
RMS_EPS = 1e-6

MLA_HEADS = 8
MLA_NOPE = 128
MLA_ROPE = 64
MLA_V = 128
MLA_Q_LORA = 384
MLA_KV_LORA = 256
ROPE_THETA = 10000.0
MLA_QK = 256

POOL_WINDOWS = (2, 4, 8, 16)
POOL_HALO = 16

GDN_QK_HEADS = 8
GDN_V_HEADS = 16
GDN_DK = 128
GDN_DV = 128
GDN_CONV = 4
GDN_Q_DIM = GDN_QK_HEADS * GDN_DK
GDN_V_DIM = GDN_V_HEADS * GDN_DV
GDN_CONV_DIM = 2 * GDN_Q_DIM + GDN_V_DIM
GDN_HALO = 8

N_EXPERTS = 32
TOP_K = 4
SWIGLU_LIMIT = 7.0
SWIGLU_ALPHA = 1.702

VMEM_LIMIT = 52 * 1024 * 1024
ROW_TILE = 8


def _cparams(sem):
    return pltpu.CompilerParams(dimension_semantics=sem, vmem_limit_bytes=VMEM_LIMIT)


def _bdot(a, b):
    return jnp.dot(a.astype(BF16), b.astype(BF16), preferred_element_type=F32)


def _bdot_nt(a, b):
    return lax.dot_general(a.astype(BF16), b.astype(BF16), (((1,), (1,)), ((), ())),
                           preferred_element_type=F32)


def _bdot_tn(a, b):
    return lax.dot_general(a.astype(BF16), b.astype(BF16), (((0,), (0,)), ((), ())),
                           preferred_element_type=F32)


def _split(a):
    hi = a.astype(BF16)
    lo = (a - hi.astype(F32)).astype(BF16)
    return hi, lo


def _dot3(a, b):
    ah, al = _split(a)
    bh, bl = _split(b)
    d = functools.partial(jnp.dot, preferred_element_type=F32)
    return d(ah, bh) + (d(ah, bl) + d(al, bh))


def _dot3_nt(a, b):
    ah, al = _split(a)
    bh, bl = _split(b)
    d = lambda x, y: lax.dot_general(x, y, (((1,), (1,)), ((), ())), preferred_element_type=F32)
    return d(ah, bh) + (d(ah, bl) + d(al, bh))


def _layer_norm(z, g, b):
    mu = jnp.mean(z, axis=-1, keepdims=True)
    zc = z - mu
    var = jnp.mean(zc * zc, axis=-1, keepdims=True)
    return zc * lax.rsqrt(var + LN_EPS) * g + b


def _sigmoid(x):
    return 1.0 / (1.0 + jnp.exp(-x))


def _silu(x):
    return x * _sigmoid(x)


def _ada_kernel(c_ref, w_ref, b_ref, o_ref):
    sc = _silu(c_ref[...])
    o_ref[...] = _dot3(sc, w_ref[...]) + b_ref[...]


def _ada_mods(c, ada_w, ada_b):
    depth, d, n = ada_w.shape
    bsz = c.shape[0]
    rows = 8
    tn = 2048
    c8 = jnp.zeros((rows, d), F32).at[:bsz].set(c)
    out = pl.pallas_call(
        _ada_kernel,
        grid=(depth, n // tn),
        in_specs=[
            pl.BlockSpec((rows, d), lambda l, j: (0, 0)),
            pl.BlockSpec((None, d, tn), lambda l, j: (l, 0, j)),
            pl.BlockSpec((None, 1, tn), lambda l, j: (l, 0, j)),
        ],
        out_specs=pl.BlockSpec((None, rows, tn), lambda l, j: (l, 0, j)),
        out_shape=jax.ShapeDtypeStruct((depth, rows, n), F32),
        compiler_params=_cparams(("arbitrary", "arbitrary")),
        name="ada_mods",
    )(c8, ada_w, ada_b.reshape(depth, 1, n))
    return out[:, :bsz].reshape(depth, bsz, 6, d)


def _route_io(bsz, s, d, tm):
    t = bsz * s
    nst = s // tm
    ne = N_EXPERTS
    in_specs = [pl.BlockSpec((ne, d), lambda b, i: (0, 0)), pl.BlockSpec((ne, 1), lambda b, i: (0, 0))]
    out_specs = [pl.BlockSpec((TOP_K, tm), lambda b, i: (0, b * nst + i))] * 3 + [
        pl.BlockSpec((ne, 128), lambda b, i: (0, 0))]
    out_shape = [jax.ShapeDtypeStruct((TOP_K, t), jnp.int32), jax.ShapeDtypeStruct((TOP_K, t), F32),
                 jax.ShapeDtypeStruct((TOP_K, t), jnp.int32), jax.ShapeDtypeStruct((ne, 128), F32)]
    return in_specs, out_specs, out_shape, [pltpu.VMEM((ne, 128), F32)]


def _post_kernel(alpha, gate_row, tm, y_ref, w_ref, x_ref, mod_ref, g_ref, b_ref, wr_ref, br_ref,
                 o_ref, idx_ref, gate_ref, rank_ref, cnt_ref, run):
    y = jnp.dot(y_ref[...], w_ref[...], preferred_element_type=F32)
    gate = mod_ref[gate_row:gate_row + 1, :]
    z = alpha * x_ref[...] + (1.0 + gate) * y
    x_new = _layer_norm(z, g_ref[...], b_ref[...])
    o_ref[...] = x_new
    _route_tile(tm, x_new, mod_ref, wr_ref, br_ref, idx_ref, gate_ref, rank_ref, cnt_ref, run)


def _post(y_pre, w_bf, x, mods_i, ln_g, ln_b, alpha, gate_row, w_router, b_router, tm=512):
    bsz, s, d = x.shape
    kd = y_pre.shape[-1]
    r_in, r_out, r_shape, r_scratch = _route_io(bsz, s, d, tm)
    outs = pl.pallas_call(
        functools.partial(_post_kernel, alpha, gate_row, tm),
        grid=(bsz, s // tm),
        in_specs=[
            pl.BlockSpec((None, tm, kd), lambda b, i: (b, i, 0)),
            pl.BlockSpec((kd, d), lambda b, i: (0, 0)),
            pl.BlockSpec((None, tm, d), lambda b, i: (b, i, 0)),
            pl.BlockSpec((None, 6, d), lambda b, i: (b, 0, 0)),
            pl.BlockSpec((1, d), lambda b, i: (0, 0)),
            pl.BlockSpec((1, d), lambda b, i: (0, 0)),
        ] + r_in,
        out_specs=[pl.BlockSpec((None, tm, d), lambda b, i: (b, i, 0))] + r_out,
        out_shape=[jax.ShapeDtypeStruct((bsz, s, d), F32)] + r_shape,
        scratch_shapes=r_scratch,
        compiler_params=_cparams(("arbitrary", "arbitrary")),
        name="post_proj_ln",
    )(y_pre, w_bf, x, mods_i, ln_g.reshape(1, d), ln_b.reshape(1, d), w_router.T, b_router.reshape(-1, 1))
    return outs[0], tuple(outs[1:])


def _mla_proj_kernel(qscale, x_ref, mod_ref, wd_ref, gq_ref, gkv_ref, wuqt_ref, wk_ref, wvt_ref, cos_ref, sin_ref,
                     cost_ref, sint_ref, qt_ref, k_ref, vt_ref):
    shift = mod_ref[0:1, :]
    scale = mod_ref[1:2, :]
    h = x_ref[...] * (1.0 + scale) + shift
    down = _bdot(h, wd_ref[...])
    cq = down[:, :MLA_Q_LORA]
    ckv = down[:, MLA_Q_LORA:MLA_Q_LORA + MLA_KV_LORA]
    kra = down[:, 640:768]
    krb = down[:, 768:896]
    cos = cos_ref[...]
    sin = sin_ref[...]
    k_rope = (kra * cos + krb * sin).astype(BF16)

    cqn = cq * lax.rsqrt(jnp.mean(cq * cq, axis=-1, keepdims=True) + RMS_EPS) * gq_ref[...]
    ckvn = ckv * lax.rsqrt(jnp.mean(ckv * ckv, axis=-1, keepdims=True) + RMS_EPS) * gkv_ref[...]
    cqn = cqn.astype(BF16)
    ckvn = ckvn.astype(BF16)
    cost = cost_ref[...]
    sint = sint_ref[...]
    nh = MLA_HEADS

    def nt(a, b):
        return lax.dot_general(a, b, (((1,), (1,)), ((), ())), preferred_element_type=F32)

    for hh in range(nh):
        qnt = nt(wuqt_ref[hh * 128:(hh + 1) * 128, :], cqn)
        qat = nt(wuqt_ref[(nh + hh) * 128:(nh + hh + 1) * 128, :], cqn)
        qbt = nt(wuqt_ref[(2 * nh + hh) * 128:(2 * nh + hh + 1) * 128, :], cqn)
        qt_ref[hh, 0:128, :] = (qnt * qscale).astype(BF16)
        qt_ref[hh, 128:256, :] = ((qat * cost + qbt * sint) * qscale).astype(BF16)
        kn = jnp.dot(ckvn, wk_ref[:, hh * 128:(hh + 1) * 128], preferred_element_type=F32)
        k_ref[hh, :, 0:128] = kn.astype(BF16)
        k_ref[hh, :, 128:256] = k_rope
        vt_ref[hh, :, :] = nt(wvt_ref[hh * 128:(hh + 1) * 128, :], ckvn).astype(BF16)


MLA_HPS = 2


def _mla_attn_kernel(tq, tk, qt_ref, k_ref, vt_ref, o_ref, sbuf):
    assert tk == tq
    i = pl.program_id(2)
    nhs = MLA_HPS
    hq = tq // 2
    regions = [(h, half) for h in range(nhs) for half in range(2)]
    qts = {(h, half): qt_ref[h, :, half * hq:(half + 1) * hq] for h, half in regions}

    def scores(h, half, j):
        start = pl.multiple_of(j * tk, tk)
        return jnp.dot(k_ref[h, pl.ds(start, tk), :], qts[(h, half)], preferred_element_type=F32)

    def consume(h, half, j, st):
        m_old, l_old, acc = st
        s = sbuf[h, :, half * hq:(half + 1) * hq]
        m_new = jnp.maximum(m_old, jnp.max(s, axis=0, keepdims=True))
        p = jnp.exp2(s - m_new)
        corr = jnp.exp2(m_old - m_new)
        l_new = corr * l_old + jnp.sum(p, axis=0, keepdims=True)
        start = pl.multiple_of(j * tk, tk)
        pv = jnp.dot(vt_ref[h, :, pl.ds(start, tk)], p.astype(BF16), preferred_element_type=F32)
        return m_new, l_new, corr * acc + pv

    for h, half in regions:
        kc = (i * tk + lax.broadcasted_iota(jnp.int32, (tk, hq), 0)) // CHUNK
        qc = (i * tq + half * hq + lax.broadcasted_iota(jnp.int32, (tk, hq), 1)) // CHUNK
        sbuf[h, :, half * hq:(half + 1) * hq] = jnp.where(kc <= qc, scores(h, half, i), -1e30)

    def body(jj, carry):
        j_old = jnp.where(jj == 0, i, jj - 1)
        out = []
        for r, (h, half) in enumerate(regions):
            s_new = scores(h, half, jj)
            out.append(consume(h, half, j_old, carry[r]))
            sbuf[h, :, half * hq:(half + 1) * hq] = s_new
        return tuple(out)

    init = tuple((jnp.full((1, hq), -1e30, F32), jnp.zeros((1, hq), F32), jnp.zeros((MLA_V, hq), F32))
                 for _ in regions)
    st = lax.fori_loop(0, i, body, init)
    j_last = jnp.maximum(i - 1, 0)
    for r, (h, half) in enumerate(regions):
        _, l_fin, acc = consume(h, half, j_last, st[r])
        ot = acc / l_fin
        o_ref[half * hq:(half + 1) * hq, h * MLA_V:(h + 1) * MLA_V] = ot.T.astype(BF16)


def _mla_layer(x, mods_i, w_dqkv, g_q, g_kv, w_uq, w_ukv, w_o, ln_g, ln_b, alpha, w_router, b_router,
               tm=512, tq=512, tk=512):
    bsz, s, d = x.shape
    nh = MLA_HEADS
    half = MLA_ROPE // 2
    k1 = w_dqkv[:, 640:640 + half]
    k2 = w_dqkv[:, 640 + half:704]
    z64 = jnp.zeros((d, 64), F32)
    wd = jnp.concatenate([w_dqkv[:, :640], k1, k2, z64, k2, k1, z64], axis=1).astype(BF16)
    wq = w_uq.reshape(MLA_Q_LORA, nh, MLA_NOPE + MLA_ROPE)
    wq_n = wq[:, :, :MLA_NOPE].reshape(MLA_Q_LORA, nh * 128)
    x1 = wq[:, :, MLA_NOPE:MLA_NOPE + half]
    x2 = wq[:, :, MLA_NOPE + half:]
    zq = jnp.zeros((MLA_Q_LORA, nh, 64), F32)
    wq_a = jnp.concatenate([x1, x2, zq], axis=2).reshape(MLA_Q_LORA, nh * 128)
    wq_b = jnp.concatenate([x2, x1, zq], axis=2).reshape(MLA_Q_LORA, nh * 128)
    wuqt = jnp.concatenate([wq_n, wq_a, wq_b], axis=1).T.astype(BF16)
    wkv = w_ukv.reshape(MLA_KV_LORA, nh, MLA_NOPE + MLA_V)
    wk = wkv[:, :, :MLA_NOPE].reshape(MLA_KV_LORA, nh * MLA_NOPE).astype(BF16)
    wvt = wkv[:, :, MLA_NOPE:].reshape(MLA_KV_LORA, nh * MLA_V).T.astype(BF16)
    pos = jnp.arange(s, dtype=F32)
    inv_freq = ROPE_THETA ** (-jnp.arange(0, MLA_ROPE, 2, dtype=F32) / MLA_ROPE)
    ang = pos[:, None] * inv_freq[None, :]
    cs, sn = jnp.cos(ang), jnp.sin(ang)
    zt = jnp.zeros((s, 64), F32)
    cos_t = jnp.concatenate([cs, cs, zt], axis=1)
    sin_t = jnp.concatenate([-sn, sn, zt], axis=1)
    qscale = (MLA_NOPE + MLA_ROPE) ** -0.5 * math.log2(math.e)

    qt, k, vt = pl.pallas_call(
        functools.partial(_mla_proj_kernel, qscale),
        grid=(bsz, s // tm),
        in_specs=[
            pl.BlockSpec((None, tm, d), lambda b, i: (b, i, 0)),
            pl.BlockSpec((None, 6, d), lambda b, i: (b, 0, 0)),
            pl.BlockSpec(wd.shape, lambda b, i: (0, 0)),
            pl.BlockSpec((1, MLA_Q_LORA), lambda b, i: (0, 0)),
            pl.BlockSpec((1, MLA_KV_LORA), lambda b, i: (0, 0)),
            pl.BlockSpec(wuqt.shape, lambda b, i: (0, 0)),
            pl.BlockSpec(wk.shape, lambda b, i: (0, 0)),
            pl.BlockSpec(wvt.shape, lambda b, i: (0, 0)),
            pl.BlockSpec((tm, 128), lambda b, i: (i, 0)),
            pl.BlockSpec((tm, 128), lambda b, i: (i, 0)),
            pl.BlockSpec((128, tm), lambda b, i: (0, i)),
            pl.BlockSpec((128, tm), lambda b, i: (0, i)),
        ],
        out_specs=[
            pl.BlockSpec((None, nh, MLA_QK, tm), lambda b, i: (b, 0, 0, i)),
            pl.BlockSpec((None, nh, tm, MLA_QK), lambda b, i: (b, 0, i, 0)),
            pl.BlockSpec((None, nh, MLA_V, tm), lambda b, i: (b, 0, 0, i)),
        ],
        out_shape=[
            jax.ShapeDtypeStruct((bsz, nh, MLA_QK, s), BF16),
            jax.ShapeDtypeStruct((bsz, nh, s, MLA_QK), BF16),
            jax.ShapeDtypeStruct((bsz, nh, MLA_V, s), BF16),
        ],
        compiler_params=_cparams(("arbitrary", "arbitrary")),
        name="mla_proj",
    )(x, mods_i, wd, g_q.reshape(1, -1), g_kv.reshape(1, -1), wuqt, wk, wvt, cos_t, sin_t, cos_t.T, sin_t.T)

    o = pl.pallas_call(
        functools.partial(_mla_attn_kernel, tq, tk),
        grid=(bsz, nh // MLA_HPS, s // tq),
        in_specs=[
            pl.BlockSpec((None, MLA_HPS, MLA_QK, tq), lambda b, h, i: (b, h, 0, i)),
            pl.BlockSpec((None, MLA_HPS, s, MLA_QK), lambda b, h, i: (b, h, 0, 0)),
            pl.BlockSpec((None, MLA_HPS, MLA_V, s), lambda b, h, i: (b, h, 0, 0)),
        ],
        out_specs=pl.BlockSpec((None, tq, MLA_HPS * MLA_V), lambda b, h, i: (b, i, h)),
        out_shape=jax.ShapeDtypeStruct((bsz, s, nh * MLA_V), BF16),
        scratch_shapes=[pltpu.VMEM((MLA_HPS, tk, tq), F32)],
        compiler_params=_cparams(("arbitrary", "arbitrary", "arbitrary")),
        name="mla_attn",
    )(qt, k, vt)

    return _post(o, w_o.astype(BF16), x, mods_i, ln_g, ln_b, alpha, 2, w_router, b_router)


def _pool_kernel(alpha, tm, x_ref, halo_ref, mod_ref, w_ref, cs_ref, g_ref, b_ref, wr_ref, br_ref,
                 o_ref, idx_ref, gate_ref, rank_ref, cnt_ref, buf, run):
    i = pl.program_id(1)
    d = x_ref.shape[-1]
    gd = d // len(POOL_WINDOWS)
    shift = mod_ref[0:1, :]
    scale = mod_ref[1:2, :]
    gate = mod_ref[2:3, :]
    x = x_ref[...]
    h = x * (1.0 + scale) + shift
    hh = halo_ref[...] * (1.0 + scale) + shift
    hh = jnp.where(i > 0, hh, 0.0)
    buf[0:POOL_HALO, :] = hh
    buf[POOL_HALO:, :] = h
    t = i * tm + lax.broadcasted_iota(jnp.int32, (tm, 1), 0)
    ys = []
    for g, win in enumerate(POOL_WINDOWS):
        cols = slice(g * gd, (g + 1) * gd)
        acc = buf[POOL_HALO:POOL_HALO + tm, cols]
        for j in range(1, win):
            acc = acc + buf[POOL_HALO - j:POOL_HALO - j + tm, cols]
        count = jnp.minimum(t + 1, win).astype(F32)
        dg = acc / count - h[:, cols]
        ys.append(_bdot(dg, w_ref[g]))
    y = jnp.concatenate(ys, axis=-1) * cs_ref[...]
    z = alpha * x + (1.0 + gate) * y
    x_new = _layer_norm(z, g_ref[...], b_ref[...])
    o_ref[...] = x_new
    _route_tile(tm, x_new, mod_ref, wr_ref, br_ref, idx_ref, gate_ref, rank_ref, cnt_ref, run)


def _pool_layer(x, mods_i, w_pool, ch_scale, ln_g, ln_b, alpha, w_router, b_router, tm=512):
    bsz, s, d = x.shape
    gd = d // len(POOL_WINDOWS)
    hb = tm // POOL_HALO
    r_in, r_out, r_shape, r_scratch = _route_io(bsz, s, d, tm)
    outs = pl.pallas_call(
        functools.partial(_pool_kernel, alpha, tm),
        grid=(bsz, s // tm),
        in_specs=[
            pl.BlockSpec((None, tm, d), lambda b, i: (b, i, 0)),
            pl.BlockSpec((None, POOL_HALO, d), lambda b, i: (b, jnp.maximum(i * hb - 1, 0), 0)),
            pl.BlockSpec((None, 6, d), lambda b, i: (b, 0, 0)),
            pl.BlockSpec((len(POOL_WINDOWS), gd, gd), lambda b, i: (0, 0, 0)),
            pl.BlockSpec((1, d), lambda b, i: (0, 0)),
            pl.BlockSpec((1, d), lambda b, i: (0, 0)),
            pl.BlockSpec((1, d), lambda b, i: (0, 0)),
        ] + r_in,
        out_specs=[pl.BlockSpec((None, tm, d), lambda b, i: (b, i, 0))] + r_out,
        out_shape=[jax.ShapeDtypeStruct((bsz, s, d), F32)] + r_shape,
        scratch_shapes=[pltpu.VMEM((tm + POOL_HALO, d), F32)] + r_scratch,
        compiler_params=_cparams(("arbitrary", "arbitrary")),
        name="pool_layer",
    )(x, x, mods_i, w_pool.astype(BF16), ch_scale.reshape(1, d), ln_g.reshape(1, d), ln_b.reshape(1, d),
      w_router.T, b_router.reshape(-1, 1))
    return outs[0], tuple(outs[1:])


def _gdn_proj_kernel(tm, x_ref, halo_ref, mod_ref, wqkv_ref, wz_ref, wba_ref, wc_ref, alog_ref, dtb_ref,
                     q_ref, k_ref, v_ref, z_ref, beta_ref, gc_ref, hbuf, pbuf):
    i = pl.program_id(1)
    shift = mod_ref[0:1, :]
    scale = mod_ref[1:2, :]
    h = x_ref[...] * (1.0 + scale) + shift
    hh = halo_ref[...] * (1.0 + scale) + shift
    hh = jnp.where(i > 0, hh, 0.0)
    hbuf[0:GDN_HALO, :] = hh
    hbuf[GDN_HALO:, :] = h
    h_ext = hbuf[...].astype(BF16)

    z_ref[...] = jnp.dot(h.astype(BF16), wz_ref[...], preferred_element_type=F32).astype(BF16)

    ba = _dot3(h, wba_ref[...])
    nv = GDN_V_HEADS
    beta_ref[...] = _sigmoid(ba[:, :nv])
    a = ba[:, nv:] + dtb_ref[...]
    softplus = jnp.maximum(a, 0.0) + jnp.log(1.0 + jnp.exp(-jnp.abs(a)))
    g = -jnp.exp(alog_ref[...]) * softplus
    r = lax.broadcasted_iota(jnp.int32, (tm, tm), 0)
    c = lax.broadcasted_iota(jnp.int32, (tm, tm), 1)
    tri = jnp.where((c <= r) & (c // CHUNK == r // CHUNK), 1.0, 0.0).astype(BF16)
    g_hi = g.astype(BF16)
    g_lo = (g - g_hi.astype(F32)).astype(BF16)
    g_lo2 = (g - g_hi.astype(F32) - g_lo.astype(F32)).astype(BF16)
    d = functools.partial(jnp.dot, preferred_element_type=F32)
    gc_ref[...] = d(tri, g_hi) + (d(tri, g_lo) + d(tri, g_lo2))

    cw = 512
    for cb in range(GDN_CONV_DIM // cw):
        cols = slice(cb * cw, (cb + 1) * cw)
        pbuf[...] = jnp.dot(h_ext, wqkv_ref[:, cols], preferred_element_type=F32)
        acc = pbuf[GDN_HALO:GDN_HALO + tm, :] * wc_ref[GDN_CONV - 1:GDN_CONV, cols]
        for j in range(GDN_CONV - 1):
            off = GDN_HALO - (GDN_CONV - 1) + j
            acc = acc + pbuf[off:off + tm, :] * wc_ref[j:j + 1, cols]
        y = _silu(acc)
        if cb * cw < 2 * GDN_Q_DIM:
            outs = []
            for hd in range(cw // GDN_DK):
                yh = y[:, hd * GDN_DK:(hd + 1) * GDN_DK]
                outs.append(yh * lax.rsqrt(jnp.sum(yh * yh, axis=-1, keepdims=True) + RMS_EPS))
            y = jnp.concatenate(outs, axis=-1)
            if cb * cw < GDN_Q_DIM:
                q_ref[:, cols] = (y * (GDN_DK ** -0.5)).astype(BF16)
            else:
                k_ref[:, cb * cw - GDN_Q_DIM:(cb + 1) * cw - GDN_Q_DIM] = y.astype(BF16)
        else:
            v_ref[:, cb * cw - 2 * GDN_Q_DIM:(cb + 1) * cw - 2 * GDN_Q_DIM] = y.astype(BF16)


def _col(x, hidx):
    lane = lax.broadcasted_iota(jnp.int32, x.shape, 1)
    return jnp.sum(jnp.where(lane == hidx, x, 0.0), axis=-1, keepdims=True)


GDN_HPS = 4


def _tile_rows(x, n):
    return jnp.concatenate([x] * n, axis=0)


def _gdn_chunk_kernel(tb, q_ref, k_ref, v_ref, z_ref, beta_ref, gc_ref, gr_ref, gn_ref, o_ref, s01, s23):
    hg = pl.program_id(1)
    n = pl.program_id(2)
    cs = CHUNK
    hp = GDN_HPS
    dv = GDN_DV
    wp = hp * cs
    wv = hp * dv

    @pl.when(n == 0)
    def _():
        s01[...] = jnp.zeros_like(s01)
        s23[...] = jnp.zeros_like(s23)

    bcols = [_col(beta_ref[...], hp * hg + i) for i in range(hp)]
    gcols = [_col(gc_ref[...], hp * hg + i) for i in range(hp)]
    beta_b = jnp.concatenate([jnp.broadcast_to(cc, (tb, dv)) for cc in bcols], axis=1)
    g_b = jnp.concatenate([jnp.broadcast_to(cc, (tb, dv)) for cc in gcols], axis=1)
    lane_p = lax.broadcasted_iota(jnp.int32, (tb, wp), 1)
    g_p = jnp.broadcast_to(gcols[hp - 1], (tb, wp))
    for i in reversed(range(hp - 1)):
        g_p = jnp.where(lane_p < (i + 1) * cs, gcols[i], g_p)

    r = lax.broadcasted_iota(jnp.int32, (cs, wp), 0)
    lc = lax.broadcasted_iota(jnp.int32, (cs, wp), 1) & (cs - 1)
    tril_incl = r >= lc
    tril_strict = r > lc
    eye_p = jnp.where(r == lc, 1.0, 0.0).astype(F32)
    rr = lax.broadcasted_iota(jnp.int32, (wp, wp), 0)
    ll = lax.broadcasted_iota(jnp.int32, (wp, wp), 1)
    bd_mask = (rr // cs) == (ll // cs)
    pair_mask = (rr // dv) == (ll // dv)
    rv = lax.broadcasted_iota(jnp.int32, (wp, wv), 0)
    lv = lax.broadcasted_iota(jnp.int32, (wp, wv), 1)
    bdv_mask = (rv // cs) == (lv // dv)

    def bd(y):
        return jnp.where(bd_mask, _tile_rows(y, hp), 0.0).astype(BF16)

    def bdv(y):
        return jnp.where(bdv_mask, _tile_rows(y, hp), 0.0).astype(BF16)

    def mm(a, b):
        return jnp.dot(a.astype(BF16), b, preferred_element_type=F32)

    def mm_nt(a, b):
        return lax.dot_general(a.astype(BF16), b, (((1,), (1,)), ((), ())), preferred_element_type=F32)

    states = [s01[...], s23[...]]
    gn = jnp.concatenate([gn_ref[...]] * hp, axis=1)
    nch = tb // cs
    pre = []
    for c in range(nch):
        rows = slice(c * cs, (c + 1) * cs)
        q2 = q_ref[rows, :].astype(F32)
        k2 = k_ref[rows, :].astype(F32)
        q4 = jnp.concatenate([q2[:, :dv], q2[:, :dv], q2[:, dv:], q2[:, dv:]], axis=1)
        k4 = jnp.concatenate([k2[:, :dv], k2[:, :dv], k2[:, dv:], k2[:, dv:]], axis=1)
        v4 = v_ref[rows, :].astype(F32)
        bb = beta_b[rows, :]
        gb = g_b[rows, :]
        gp = g_p[rows, :]
        grow = gr_ref[c:c + 1, :]
        decay = jnp.exp(jnp.where(tril_incl, gp - grow, -jnp.inf))
        eg = jnp.exp(gb)
        glast = gb[cs - 1:cs, :]
        kb = k4 * bb
        vb = v4 * bb
        kbd = bdv(k4)
        kk = mm_nt(kb, kbd)
        xm = -jnp.where(tril_strict, kk * decay, 0.0)
        a_intra = mm_nt(q4, kbd) * decay
        pre.append(dict(vbd=bdv(vb), kgd=bdv(kb * eg), a=a_intra, qg=q4 * eg, kd=k4 * jnp.exp(glast - gb),
                        gl=jnp.exp(glast), xm=xm))
    tinvs = [eye_p + pc["xm"] for pc in pre]
    pws = [pc["xm"] for pc in pre]
    bdps = [bd(pw) for pw in pws]
    for _ in range(5):
        pws = [mm(pw, bdp) for pw, bdp in zip(pws, bdps)]
        bdps = [bd(pw) for pw in pws]
        tinvs = [tinv + mm(tinv, bdp) for tinv, bdp in zip(tinvs, bdps)]
    us = [mm(tinv, pc["vbd"]) for tinv, pc in zip(tinvs, pre)]
    ws = [mm(tinv, pc["kgd"]) for tinv, pc in zip(tinvs, pre)]
    for c in range(nch):
        rows = slice(c * cs, (c + 1) * cs)
        u, w, a_intra, qg, kd, gl = us[c], ws[c], pre[c]["a"], pre[c]["qg"], pre[c]["kd"], pre[c]["gl"]
        vn, oq = [], []
        for p in range(2):
            sl = slice(p * 2 * dv, (p + 1) * 2 * dv)
            sb = states[p].astype(BF16)
            vn.append(u[:, sl] - mm(w[:, sl], sb))
            oq.append(mm(qg[:, sl], sb))
        v_new = jnp.concatenate(vn, axis=1)
        o = jnp.concatenate(oq, axis=1) + mm(a_intra, bdv(v_new))
        for p in range(2):
            sl = slice(p * 2 * dv, (p + 1) * 2 * dv)
            upd = lax.dot_general(kd[:, sl].astype(BF16), v_new[:, sl].astype(BF16), (((0,), (0,)), ((), ())),
                                  preferred_element_type=F32)
            states[p] = states[p] * gl[:, sl] + jnp.where(pair_mask, upd, 0.0)
        outs = []
        for i in range(hp):
            oi = o[:, i * dv:(i + 1) * dv]
            outs.append(oi * lax.rsqrt(jnp.mean(oi * oi, axis=-1, keepdims=True) + RMS_EPS))
        on = jnp.concatenate(outs, axis=1) * gn
        zz = z_ref[rows, :].astype(F32)
        o_ref[rows, :] = (on * _silu(zz)).astype(BF16)
    s01[...] = states[0]
    s23[...] = states[1]


def _gdn_layer(x, mods_i, w_in, w_conv, a_log, dt_bias, g_norm, w_o, ln_g, ln_b, alpha, w_router, b_router,
               tm=256, tb=512):
    bsz, s, d = x.shape
    nv = GDN_V_HEADS
    wqkv = w_in[:, :GDN_CONV_DIM].astype(BF16)
    wz = w_in[:, GDN_CONV_DIM:GDN_CONV_DIM + GDN_V_DIM].astype(BF16)
    wba = w_in[:, GDN_CONV_DIM + GDN_V_DIM:]
    hb = tm // GDN_HALO
    q, k, v, z, beta, gc = pl.pallas_call(
        functools.partial(_gdn_proj_kernel, tm),
        grid=(bsz, s // tm),
        in_specs=[
            pl.BlockSpec((None, tm, d), lambda b, i: (b, i, 0)),
            pl.BlockSpec((None, GDN_HALO, d), lambda b, i: (b, jnp.maximum(i * hb - 1, 0), 0)),
            pl.BlockSpec((None, 6, d), lambda b, i: (b, 0, 0)),
            pl.BlockSpec(wqkv.shape, lambda b, i: (0, 0)),
            pl.BlockSpec(wz.shape, lambda b, i: (0, 0)),
            pl.BlockSpec(wba.shape, lambda b, i: (0, 0)),
            pl.BlockSpec((GDN_CONV, GDN_CONV_DIM), lambda b, i: (0, 0)),
            pl.BlockSpec((1, nv), lambda b, i: (0, 0)),
            pl.BlockSpec((1, nv), lambda b, i: (0, 0)),
        ],
        out_specs=[
            pl.BlockSpec((None, tm, GDN_Q_DIM), lambda b, i: (b, i, 0)),
            pl.BlockSpec((None, tm, GDN_Q_DIM), lambda b, i: (b, i, 0)),
            pl.BlockSpec((None, tm, GDN_V_DIM), lambda b, i: (b, i, 0)),
            pl.BlockSpec((None, tm, GDN_V_DIM), lambda b, i: (b, i, 0)),
            pl.BlockSpec((None, tm, nv), lambda b, i: (b, i, 0)),
            pl.BlockSpec((None, tm, nv), lambda b, i: (b, i, 0)),
        ],
        out_shape=[
            jax.ShapeDtypeStruct((bsz, s, GDN_Q_DIM), BF16),
            jax.ShapeDtypeStruct((bsz, s, GDN_Q_DIM), BF16),
            jax.ShapeDtypeStruct((bsz, s, GDN_V_DIM), BF16),
            jax.ShapeDtypeStruct((bsz, s, GDN_V_DIM), BF16),
            jax.ShapeDtypeStruct((bsz, s, nv), F32),
            jax.ShapeDtypeStruct((bsz, s, nv), F32),
        ],
        scratch_shapes=[pltpu.VMEM((tm + GDN_HALO, d), F32), pltpu.VMEM((tm + GDN_HALO, 512), F32)],
        compiler_params=_cparams(("arbitrary", "arbitrary")),
        name="gdn_proj",
    )(x, x, mods_i, wqkv, wz, wba, w_conv, a_log.reshape(1, nv), dt_bias.reshape(1, nv))

    hp = GDN_HPS
    nc = s // CHUNK
    gr = gc.reshape(bsz, nc, CHUNK, nv // hp, hp).transpose(0, 3, 1, 4, 2).reshape(bsz, nv // hp, nc, hp * CHUNK)
    rep = GDN_V_HEADS // GDN_QK_HEADS
    cpb = tb // CHUNK
    qw = hp // rep * GDN_DK
    o = pl.pallas_call(
        functools.partial(_gdn_chunk_kernel, tb),
        grid=(bsz, nv // hp, s // tb),
        in_specs=[
            pl.BlockSpec((None, tb, qw), lambda b, h, n: (b, n, h)),
            pl.BlockSpec((None, tb, qw), lambda b, h, n: (b, n, h)),
            pl.BlockSpec((None, tb, hp * GDN_DV), lambda b, h, n: (b, n, h)),
            pl.BlockSpec((None, tb, hp * GDN_DV), lambda b, h, n: (b, n, h)),
            pl.BlockSpec((None, tb, nv), lambda b, h, n: (b, n, 0)),
            pl.BlockSpec((None, tb, nv), lambda b, h, n: (b, n, 0)),
            pl.BlockSpec((None, None, cpb, hp * CHUNK), lambda b, h, n: (b, h, n, 0)),
            pl.BlockSpec((1, GDN_DV), lambda b, h, n: (0, 0)),
        ],
        out_specs=pl.BlockSpec((None, tb, hp * GDN_DV), lambda b, h, n: (b, n, h)),
        out_shape=jax.ShapeDtypeStruct((bsz, s, GDN_V_DIM), BF16),
        scratch_shapes=[pltpu.VMEM((2 * GDN_DK, 2 * GDN_DV), F32), pltpu.VMEM((2 * GDN_DK, 2 * GDN_DV), F32)],
        compiler_params=_cparams(("arbitrary", "arbitrary", "arbitrary")),
        name="gdn_chunk",
    )(q, k, v, z, beta, gc, gr, g_norm.reshape(1, GDN_DV))

    return _post(o, w_o.astype(BF16), x, mods_i, ln_g, ln_b, alpha, 2, w_router, b_router)


def _route_tile(tm, x_new, mod_ref, wr_ref, br_ref, idx_ref, gate_ref, rank_ref, cnt_ref, run):
    step = pl.program_id(0) * pl.num_programs(1) + pl.program_id(1)

    @pl.when(step == 0)
    def _():
        run[...] = jnp.zeros_like(run)

    shift = mod_ref[3:4, :]
    scale = mod_ref[4:5, :]
    h = x_new * (1.0 + scale) + shift
    logits = _dot3_nt(wr_ref[...], h) + br_ref[...]
    eid = lax.broadcasted_iota(jnp.int32, (N_EXPERTS, tm), 0)
    vals, idxs, hots = [], [], []
    cur = logits
    for _ in range(TOP_K):
        m = jnp.max(cur, axis=0, keepdims=True)
        sel = jnp.min(jnp.where(cur == m, eid, N_EXPERTS), axis=0, keepdims=True)
        hot = eid == sel
        vals.append(m)
        idxs.append(sel)
        hots.append(hot)
        cur = jnp.where(hot, -jnp.inf, cur)
    es = [jnp.exp(v - vals[0]) for v in vals]
    den = es[0] + es[1] + es[2] + es[3]
    cnt = jnp.zeros((N_EXPERTS, tm), F32)
    for hot in hots:
        cnt = cnt + jnp.where(hot, 1.0, 0.0)
    r = lax.broadcasted_iota(jnp.int32, (tm, tm), 0)
    c = lax.broadcasted_iota(jnp.int32, (tm, tm), 1)
    before = jnp.where(r < c, 1.0, 0.0).astype(BF16)
    prefix = jnp.dot(cnt.astype(BF16), before, preferred_element_type=F32) + run[:, 0:1]
    for kk in range(TOP_K):
        idx_ref[kk:kk + 1, :] = idxs[kk]
        gate_ref[kk:kk + 1, :] = es[kk] / den
        rank_ref[kk:kk + 1, :] = jnp.sum(jnp.where(hots[kk], prefix, 0.0), axis=0, keepdims=True).astype(jnp.int32)
    run[...] = run[...] + jnp.sum(cnt, axis=1, keepdims=True)
    cnt_ref[...] = run[...]


def _dispatch_kernel(tm, dest_hbm, x_ref, mod_ref, zeros_hbm, rows_hbm, d_smem, hbuf, d_sem, s_sem):
    del zeros_hbm
    step = pl.program_id(0) * pl.num_programs(1) + pl.program_id(1)
    nsteps = pl.num_programs(0) * pl.num_programs(1)
    slot = step % 2
    nxt = 1 - slot

    def d_copy(blk, sl):
        return pltpu.make_async_copy(dest_hbm.at[blk], d_smem.at[sl], d_sem.at[sl])

    def wait_scatter(sl):
        for _ in range(TOP_K):
            pltpu.make_async_copy(hbuf.at[sl], rows_hbm.at[pl.ds(0, tm * ROW_TILE), :], s_sem.at[sl]).wait()

    @pl.when(step == 0)
    def _():
        d_copy(0, 0).start()

    @pl.when(step >= 2)
    def _():
        wait_scatter(slot)

    shift = mod_ref[3:4, :]
    scale = mod_ref[4:5, :]
    h = x_ref[...] * (1.0 + scale) + shift
    for sc in range(ROW_TILE):
        hbuf[slot, pl.ds(sc, tm, stride=ROW_TILE), :] = h[:, sc * 128:(sc + 1) * 128]

    d_copy(step, slot).wait()

    @pl.when(step + 1 < nsteps)
    def _():
        d_copy(step + 1, nxt).start()

    def body(r, carry):
        src = hbuf.at[slot, pl.ds(pl.multiple_of(r * ROW_TILE, ROW_TILE), ROW_TILE), :]
        for kk in range(TOP_K):
            dst = d_smem[slot, kk * tm + r]
            out = rows_hbm.at[pl.ds(pl.multiple_of(dst * ROW_TILE, ROW_TILE), ROW_TILE), :]
            pltpu.make_async_copy(src, out, s_sem.at[slot]).start(priority=kk % 2)
        return carry
    lax.fori_loop(0, tm, body, 0, unroll=4)

    @pl.when(step == nsteps - 1)
    def _():
        wait_scatter(slot)

        @pl.when(nsteps > 1)
        def _():
            wait_scatter(nxt)


def _expert_kernel(tm, be_ref, nx_ref, sl_ref, nu_ref, x_ref, wgu_hbm, bgu_ref, wd_hbm, bd_ref, y_ref,
                   wgu_f32, wd_f32, wgu_bf, wd_bf, w_sem):
    j = pl.program_id(0)
    n_used = nu_ref[0]
    changed = jnp.logical_or(j == 0, be_ref[j] != be_ref[jnp.maximum(j - 1, 0)])
    slot = sl_ref[j]

    def fetch(row, s):
        return (pltpu.make_async_copy(wgu_hbm.at[row], wgu_f32.at[s], w_sem.at[s, 0]),
                pltpu.make_async_copy(wd_hbm.at[row], wd_f32.at[s], w_sem.at[s, 1]))

    @pl.when(j == 0)
    def _():
        for cp in fetch(be_ref[0], 0):
            cp.start()

    @pl.when(jnp.logical_and(changed, j < n_used))
    def _():
        for cp in fetch(be_ref[j], slot):
            cp.wait()

        @pl.when(nx_ref[j] >= 0)
        def _():
            for cp in fetch(nx_ref[j], 1 - slot):
                cp.start()

        wgu_bf[...] = wgu_f32[slot].astype(BF16)
        wd_bf[...] = wd_f32[slot].astype(BF16)

    @pl.when(j >= n_used)
    def _():
        y_ref[...] = jnp.zeros_like(y_ref)

    @pl.when(j < n_used)
    def _():
        nsc = ROW_TILE
        xb = jnp.concatenate([x_ref[pl.ds(sc, tm, stride=nsc), :] for sc in range(nsc)], axis=-1).astype(BF16)
        gu = jnp.dot(xb, wgu_bf[...], preferred_element_type=F32) + bgu_ref[...]
        dh = gu.shape[-1] // 2
        gate = jnp.minimum(gu[:, :dh], SWIGLU_LIMIT)
        up = jnp.clip(gu[:, dh:], -SWIGLU_LIMIT, SWIGLU_LIMIT)
        act = gate * _sigmoid(SWIGLU_ALPHA * gate) * (up + 1.0)
        y = jnp.dot(act.astype(BF16), wd_bf[...], preferred_element_type=F32) + bd_ref[...]
        for sc in range(nsc):
            y_ref[pl.ds(sc, tm, stride=nsc), :] = y[:, sc * 128:(sc + 1) * 128]


def _combine_kernel(alpha, tm, dest_hbm, y_hbm, x_ref, gate_ref, mod_ref, g_ref, b_ref, o_ref,
                    d_smem, ybuf, d_sem, y_sem):
    step = pl.program_id(0) * pl.num_programs(1) + pl.program_id(1)
    nsteps = pl.num_programs(0) * pl.num_programs(1)
    slot = step % 2
    nxt = 1 - slot

    def d_copy(blk, sl):
        return pltpu.make_async_copy(dest_hbm.at[blk], d_smem.at[sl], d_sem.at[sl])

    def issue_rows(sl):
        def body(r, carry):
            for kk in range(TOP_K):
                dst = d_smem[sl, kk * tm + r]
                src = y_hbm.at[pl.ds(pl.multiple_of(dst * ROW_TILE, ROW_TILE), ROW_TILE), :]
                dbuf = ybuf.at[sl, kk, pl.ds(pl.multiple_of(r * ROW_TILE, ROW_TILE), ROW_TILE), :]
                pltpu.make_async_copy(src, dbuf, y_sem.at[sl]).start(priority=kk % 2)
            return carry
        lax.fori_loop(0, tm, body, 0, unroll=4)

    @pl.when(step == 0)
    def _():
        d_copy(0, 0).start()
        d_copy(0, 0).wait()
        issue_rows(0)

        @pl.when(nsteps > 1)
        def _():
            d_copy(1, 1).start()

    @pl.when(step + 1 < nsteps)
    def _():
        d_copy(step + 1, nxt).wait()
        issue_rows(nxt)

    @pl.when(step + 2 < nsteps)
    def _():
        d_copy(step + 2, slot).start()

    for kk in range(TOP_K):
        pltpu.make_async_copy(y_hbm.at[pl.ds(0, tm * ROW_TILE), :], ybuf.at[slot, kk], y_sem.at[slot]).wait()
    g = gate_ref[...]
    parts = []
    for sc in range(ROW_TILE):
        acc = ybuf[slot, 0, pl.ds(sc, tm, stride=ROW_TILE), :] * g[:, 0:1]
        for kk in range(1, TOP_K):
            acc = acc + ybuf[slot, kk, pl.ds(sc, tm, stride=ROW_TILE), :] * g[:, kk:kk + 1]
        parts.append(acc)
    y = jnp.concatenate(parts, axis=-1)
    gate_f = mod_ref[5:6, :]
    z = alpha * x_ref[...] + (1.0 + gate_f) * y
    o_ref[...] = _layer_norm(z, g_ref[...], b_ref[...])


MOE_TE = 256


def _moe_rows_buffer(t):
    return jnp.zeros(((t * TOP_K + N_EXPERTS * MOE_TE) * ROW_TILE, 128), F32)


def _moe_layer(x, mods_i, routing, rows_buf, w_gate_up, b_gate_up, w_down, b_down, ln_g, ln_b, alpha,
               layer=0, tc=256):
    bsz, s, d = x.shape
    assert d == ROW_TILE * 128, "token rows are stored as one (8,128) f32 tile"
    t = bsz * s
    ne = N_EXPERTS
    te = MOE_TE
    idx, gates, rank, counts = routing

    cnt = counts[:, 0].astype(jnp.int32)
    padded = (cnt + te - 1) // te * te
    pend = jnp.cumsum(padded)
    pstart = pend - padded
    n_rows = t * TOP_K + ne * te
    n_blocks = n_rows // te
    eids = jnp.arange(ne, dtype=jnp.int32)
    dest = jnp.sum(jnp.where(idx[None] == eids[:, None, None], pstart[:, None, None], 0), axis=0) + rank
    n_used = (pend[-1] // te).astype(jnp.int32).reshape(1)
    blk_start = jnp.minimum(jnp.arange(n_blocks, dtype=jnp.int32) * te, pend[-1] - te)
    blk_expert = jnp.minimum(jnp.sum((blk_start[:, None] >= pend[None, :]).astype(jnp.int32), axis=1), ne - 1)
    present = padded > 0
    nxt = lax.cummin(jnp.where(present, eids, ne)[::-1])[::-1]
    next_present = jnp.concatenate([nxt[1:], jnp.full((1,), ne, jnp.int32)])
    run_id = jnp.cumsum(present.astype(jnp.int32)) - 1
    onehot = blk_expert[:, None] == eids[None, :]
    blk_next = jnp.sum(jnp.where(onehot, next_present[None, :], 0), axis=1)
    blk_slot = jnp.sum(jnp.where(onehot, run_id[None, :], 0), axis=1) % 2
    blk_next = jnp.where(blk_next < ne, blk_next + layer * ne, -1)
    blk_expert = blk_expert + layer * ne

    nct = s // tc
    dest_t = dest.reshape(TOP_K, t // tc, tc).transpose(1, 0, 2).reshape(t // tc, TOP_K * tc)
    x_rows = pl.pallas_call(
        functools.partial(_dispatch_kernel, tc),
        grid=(bsz, nct),
        in_specs=[
            pl.BlockSpec(memory_space=pl.ANY),
            pl.BlockSpec((None, tc, d), lambda b, i: (b, i, 0)),
            pl.BlockSpec((None, 6, d), lambda b, i: (b, 0, 0)),
            pl.BlockSpec(memory_space=pl.ANY),
        ],
        out_specs=pl.BlockSpec(memory_space=pl.ANY),
        scratch_shapes=[
            pltpu.SMEM((2, TOP_K * tc), jnp.int32),
            pltpu.VMEM((2, tc * ROW_TILE, 128), F32),
            pltpu.SemaphoreType.DMA((2,)),
            pltpu.SemaphoreType.DMA((2,)),
        ],
        out_shape=jax.ShapeDtypeStruct((n_rows * ROW_TILE, 128), F32),
        input_output_aliases={3: 0},
        compiler_params=_cparams(("arbitrary", "arbitrary")),
        name="moe_dispatch",
    )(dest_t, x, mods_i, rows_buf)

    dh2 = w_gate_up.shape[-1]
    w_gate_up = w_gate_up.reshape(-1, d, dh2)
    b_gate_up = b_gate_up.reshape(-1, 1, dh2)
    w_down = w_down.reshape(-1, dh2 // 2, d)
    b_down = b_down.reshape(-1, 1, d)
    y_rows = pl.pallas_call(
        functools.partial(_expert_kernel, te),
        grid_spec=pltpu.PrefetchScalarGridSpec(
            num_scalar_prefetch=4,
            grid=(n_blocks,),
            in_specs=[
                pl.BlockSpec((te * ROW_TILE, 128), lambda j, be, nx, sl, nu: (jnp.minimum(j, nu[0] - 1), 0)),
                pl.BlockSpec(memory_space=pl.ANY),
                pl.BlockSpec((None, 1, dh2), lambda j, be, nx, sl, nu: (be[j], 0, 0)),
                pl.BlockSpec(memory_space=pl.ANY),
                pl.BlockSpec((None, 1, d), lambda j, be, nx, sl, nu: (be[j], 0, 0)),
            ],
            out_specs=pl.BlockSpec((te * ROW_TILE, 128), lambda j, be, nx, sl, nu: (j, 0)),
            scratch_shapes=[
                pltpu.VMEM((2, d, dh2), F32),
                pltpu.VMEM((2, dh2 // 2, d), F32),
                pltpu.VMEM((d, dh2), BF16),
                pltpu.VMEM((dh2 // 2, d), BF16),
                pltpu.SemaphoreType.DMA((2, 2)),
            ],
        ),
        out_shape=jax.ShapeDtypeStruct((n_rows * ROW_TILE, 128), F32),
        compiler_params=_cparams(("arbitrary",)),
        name="moe_experts",
    )(blk_expert, blk_next, blk_slot, n_used, x_rows, w_gate_up, b_gate_up, w_down, b_down)

    gates_t = gates.T
    out = pl.pallas_call(
        functools.partial(_combine_kernel, alpha, tc),
        grid=(bsz, nct),
        in_specs=[
            pl.BlockSpec(memory_space=pl.ANY),
            pl.BlockSpec(memory_space=pl.ANY),
            pl.BlockSpec((None, tc, d), lambda b, i: (b, i, 0)),
            pl.BlockSpec((tc, TOP_K), lambda b, i: (b * nct + i, 0)),
            pl.BlockSpec((None, 6, d), lambda b, i: (b, 0, 0)),
            pl.BlockSpec((1, d), lambda b, i: (0, 0)),
            pl.BlockSpec((1, d), lambda b, i: (0, 0)),
        ],
        out_specs=pl.BlockSpec((None, tc, d), lambda b, i: (b, i, 0)),
        out_shape=jax.ShapeDtypeStruct((bsz, s, d), F32),
        scratch_shapes=[
            pltpu.SMEM((2, TOP_K * tc), jnp.int32),
            pltpu.VMEM((2, TOP_K, tc * ROW_TILE, 128), F32),
            pltpu.SemaphoreType.DMA((2,)),
            pltpu.SemaphoreType.DMA((2,)),
        ],
        compiler_params=_cparams(("arbitrary", "arbitrary")),
        name="moe_combine",
    )(dest_t, y_rows, x, gates_t, mods_i, ln_g.reshape(1, d), ln_b.reshape(1, d))
    return out, x_rows


def kernel(x, c, ada_w, ada_b, ln_g, ln_b, mla_w_dqkv, mla_g_q, mla_g_kv, mla_w_uq, mla_w_ukv, mla_w_o,
           pool_w, pool_scale, gdn_w_in, gdn_w_conv, gdn_a_log, gdn_dt_bias, gdn_g_norm, gdn_w_o,
           moe_w_router, moe_b_router, moe_w_gate_up, moe_b_gate_up, moe_w_down, moe_b_down):
    depth = ada_w.shape[0]
    alpha = (2.0 * depth) ** 0.25
    mods = _ada_mods(c, ada_w, ada_b)
    rows_buf = _moe_rows_buffer(x.shape[0] * x.shape[1])
    for i in range(depth):
        kind, j = i % 3, i // 3
        m_i = mods[i]
        wr, br = moe_w_router[i], moe_b_router[i]
        if kind == 0:
            x, routing = _mla_layer(x, m_i, mla_w_dqkv[j], mla_g_q[j], mla_g_kv[j], mla_w_uq[j], mla_w_ukv[j],
                                    mla_w_o[j], ln_g[i, 0], ln_b[i, 0], alpha, wr, br)
        elif kind == 1:
            x, routing = _pool_layer(x, m_i, pool_w[j], pool_scale[j], ln_g[i, 0], ln_b[i, 0], alpha, wr, br)
        else:
            x, routing = _gdn_layer(x, m_i, gdn_w_in[j], gdn_w_conv[j], gdn_a_log[j], gdn_dt_bias[j],
                                    gdn_g_norm[j], gdn_w_o[j], ln_g[i, 0], ln_b[i, 0], alpha, wr, br)
        x, rows_buf = _moe_layer(x, m_i, routing, rows_buf, moe_w_gate_up, moe_b_gate_up,
                                 moe_w_down, moe_b_down, ln_g[i, 1], ln_b[i, 1], alpha, layer=i)
    return x
```

```python
import functools
import math

import jax
import jax.numpy as jnp
from jax import lax
from jax.experimental import pallas as pl
from jax.experimental.pallas import tpu as pltpu

F32 = jnp.float32
BF16 = jnp.bfloat16

CHUNK = 64
LN_EPS = 1e-5
RMS_EPS = 1e-6

MLA_HEADS = 8
MLA_NOPE = 128
MLA_ROPE = 64
MLA_V = 128
MLA_Q_LORA = 384
MLA_KV_LORA = 256
ROPE_THETA = 10000.0
MLA_QK = 256

POOL_WINDOWS = (2, 4, 8, 16)
POOL_HALO = 16

GDN_QK_HEADS = 8
GDN_V_HEADS = 16
GDN_DK = 128
GDN_DV = 128
GDN_CONV = 4
GDN_Q_DIM = GDN_QK_HEADS * GDN_DK
GDN_V_DIM = GDN_V_HEADS * GDN_DV
GDN_CONV_DIM = 2 * GDN_Q_DIM + GDN_V_DIM
GDN_HALO = 8

N_EXPERTS = 32
TOP_K = 4
SWIGLU_LIMIT = 7.0
SWIGLU_ALPHA = 1.702

VMEM_LIMIT = 52 * 1024 * 1024
ROW_TILE = 8


def _cparams(sem):
    return pltpu.CompilerParams(dimension_semantics=sem, vmem_limit_bytes=VMEM_LIMIT)


def _bdot(a, b):
    return jnp.dot(a.astype(BF16), b.astype(BF16), preferred_element_type=F32)


def _bdot_nt(a, b):
    return lax.dot_general(a.astype(BF16), b.astype(BF16), (((1,), (1,)), ((), ())),
                           preferred_element_type=F32)


def _bdot_tn(a, b):
    return lax.dot_general(a.astype(BF16), b.astype(BF16), (((0,), (0,)), ((), ())),
                           preferred_element_type=F32)


def _split(a):
    hi = a.astype(BF16)
    lo = (a - hi.astype(F32)).astype(BF16)
    return hi, lo


def _dot3(a, b):
    ah, al = _split(a)
    bh, bl = _split(b)
    d = functools.partial(jnp.dot, preferred_element_type=F32)
    return d(ah, bh) + (d(ah, bl) + d(al, bh))


def _dot3_nt(a, b):
    ah, al = _split(a)
    bh, bl = _split(b)
    d = lambda x, y: lax.dot_general(x, y, (((1,), (1,)), ((), ())), preferred_element_type=F32)
    return d(ah, bh) + (d(ah, bl) + d(al, bh))


def _layer_norm(z, g, b):
    mu = jnp.mean(z, axis=-1, keepdims=True)
    zc = z - mu
    var = jnp.mean(zc * zc, axis=-1, keepdims=True)
    return zc * lax.rsqrt(var + LN_EPS) * g + b


def _sigmoid(x):
    return 1.0 / (1.0 + jnp.exp(-x))


def _silu(x):
    return x * _sigmoid(x)


def _ada_kernel(c_ref, w_ref, b_ref, o_ref):
    sc = _silu(c_ref[...])
    o_ref[...] = _dot3(sc, w_ref[...]) + b_ref[...]


def _ada_mods(c, ada_w, ada_b):
    depth, d, n = ada_w.shape
    bsz = c.shape[0]
    rows = 8
    tn = 2048
    c8 = jnp.zeros((rows, d), F32).at[:bsz].set(c)
    out = pl.pallas_call(
        _ada_kernel,
        grid=(depth, n // tn),
        in_specs=[
            pl.BlockSpec((rows, d), lambda l, j: (0, 0)),
            pl.BlockSpec((None, d, tn), lambda l, j: (l, 0, j)),
            pl.BlockSpec((None, 1, tn), lambda l, j: (l, 0, j)),
        ],
        out_specs=pl.BlockSpec((None, rows, tn), lambda l, j: (l, 0, j)),
        out_shape=jax.ShapeDtypeStruct((depth, rows, n), F32),
        compiler_params=_cparams(("arbitrary", "arbitrary")),
        name="ada_mods",
    )(c8, ada_w, ada_b.reshape(depth, 1, n))
    return out[:, :bsz].reshape(depth, bsz, 6, d)


def _route_io(bsz, s, d, tm):
    t = bsz * s
    nst = s // tm
    ne = N_EXPERTS
    in_specs = [pl.BlockSpec((ne, d), lambda b, i: (0, 0)), pl.BlockSpec((ne, 1), lambda b, i: (0, 0))]
    out_specs = [pl.BlockSpec((TOP_K, tm), lambda b, i: (0, b * nst + i))] * 3 + [
        pl.BlockSpec((ne, 128), lambda b, i: (0, 0))]
    out_shape = [jax.ShapeDtypeStruct((TOP_K, t), jnp.int32), jax.ShapeDtypeStruct((TOP_K, t), F32),
                 jax.ShapeDtypeStruct((TOP_K, t), jnp.int32), jax.ShapeDtypeStruct((ne, 128), F32)]
    return in_specs, out_specs, out_shape, [pltpu.VMEM((ne, 128), F32)]


def _post_kernel(alpha, gate_row, tm, y_ref, w_ref, x_ref, mod_ref, g_ref, b_ref, wr_ref, br_ref,
                 o_ref, idx_ref, gate_ref, rank_ref, cnt_ref, run):
    y = jnp.dot(y_ref[...], w_ref[...], preferred_element_type=F32)
    gate = mod_ref[gate_row:gate_row + 1, :]
    z = alpha * x_ref[...] + (1.0 + gate) * y
    x_new = _layer_norm(z, g_ref[...], b_ref[...])
    o_ref[...] = x_new
    _route_tile(tm, x_new, mod_ref, wr_ref, br_ref, idx_ref, gate_ref, rank_ref, cnt_ref, run)


def _post(y_pre, w_bf, x, mods_i, ln_g, ln_b, alpha, gate_row, w_router, b_router, tm=512):
    bsz, s, d = x.shape
    kd = y_pre.shape[-1]
    r_in, r_out, r_shape, r_scratch = _route_io(bsz, s, d, tm)
    outs = pl.pallas_call(
        functools.partial(_post_kernel, alpha, gate_row, tm),
        grid=(bsz, s // tm),
        in_specs=[
            pl.BlockSpec((None, tm, kd), lambda b, i: (b, i, 0)),
            pl.BlockSpec((kd, d), lambda b, i: (0, 0)),
            pl.BlockSpec((None, tm, d), lambda b, i: (b, i, 0)),
            pl.BlockSpec((None, 6, d), lambda b, i: (b, 0, 0)),
            pl.BlockSpec((1, d), lambda b, i: (0, 0)),
            pl.BlockSpec((1, d), lambda b, i: (0, 0)),
        ] + r_in,
        out_specs=[pl.BlockSpec((None, tm, d), lambda b, i: (b, i, 0))] + r_out,
        out_shape=[jax.ShapeDtypeStruct((bsz, s, d), F32)] + r_shape,
        scratch_shapes=r_scratch,
        compiler_params=_cparams(("arbitrary", "arbitrary")),
        name="post_proj_ln",
    )(y_pre, w_bf, x, mods_i, ln_g.reshape(1, d), ln_b.reshape(1, d), w_router.T, b_router.reshape(-1, 1))
    return outs[0], tuple(outs[1:])


def _mla_proj_kernel(qscale, x_ref, mod_ref, wd_ref, gq_ref, gkv_ref, wuqt_ref, wk_ref, wvt_ref, cos_ref, sin_ref,
                     cost_ref, sint_ref, qt_ref, k_ref, vt_ref):
    shift = mod_ref[0:1, :]
    scale = mod_ref[1:2, :]
    h = x_ref[...] * (1.0 + scale) + shift
    down = _bdot(h, wd_ref[...])
    cq = down[:, :MLA_Q_LORA]
    ckv = down[:, MLA_Q_LORA:MLA_Q_LORA + MLA_KV_LORA]
    kra = down[:, 640:768]
    krb = down[:, 768:896]
    cos = cos_ref[...]
    sin = sin_ref[...]
    k_rope = (kra * cos + krb * sin).astype(BF16)

    cqn = cq * lax.rsqrt(jnp.mean(cq * cq, axis=-1, keepdims=True) + RMS_EPS) * gq_ref[...]
    ckvn = ckv * lax.rsqrt(jnp.mean(ckv * ckv, axis=-1, keepdims=True) + RMS_EPS) * gkv_ref[...]
    cqn = cqn.astype(BF16)
    ckvn = ckvn.astype(BF16)
    cost = cost_ref[...]
    sint = sint_ref[...]
    nh = MLA_HEADS

    def nt(a, b):
        return lax.dot_general(a, b, (((1,), (1,)), ((), ())), preferred_element_type=F32)

    for hh in range(nh):
        qnt = nt(wuqt_ref[hh * 128:(hh + 1) * 128, :], cqn)
        qat = nt(wuqt_ref[(nh + hh) * 128:(nh + hh + 1) * 128, :], cqn)
        qbt = nt(wuqt_ref[(2 * nh + hh) * 128:(2 * nh + hh + 1) * 128, :], cqn)
        qt_ref[hh, 0:128, :] = (qnt * qscale).astype(BF16)
        qt_ref[hh, 128:256, :] = ((qat * cost + qbt * sint) * qscale).astype(BF16)
        kn = jnp.dot(ckvn, wk_ref[:, hh * 128:(hh + 1) * 128], preferred_element_type=F32)
        k_ref[hh, :, 0:128] = kn.astype(BF16)
        k_ref[hh, :, 128:256] = k_rope
        vt_ref[hh, :, :] = nt(wvt_ref[hh * 128:(hh + 1) * 128, :], ckvn).astype(BF16)


MLA_HPS = 2


def _mla_attn_kernel(tq, tk, qt_ref, k_ref, vt_ref, o_ref, sbuf):
    assert tk == tq
    i = pl.program_id(2)
    nhs = MLA_HPS
    hq = tq // 2
    regions = [(h, half) for h in range(nhs) for half in range(2)]
    qts = {(h, half): qt_ref[h, :, half * hq:(half + 1) * hq] for h, half in regions}

    def scores(h, half, j):
        start = pl.multiple_of(j * tk, tk)
        return jnp.dot(k_ref[h, pl.ds(start, tk), :], qts[(h, half)], preferred_element_type=F32)

    def consume(h, half, j, st):
        m_old, l_old, acc = st
        s = sbuf[h, :, half * hq:(half + 1) * hq]
        m_new = jnp.maximum(m_old, jnp.max(s, axis=0, keepdims=True))
        p = jnp.exp2(s - m_new)
        corr = jnp.exp2(m_old - m_new)
        l_new = corr * l_old + jnp.sum(p, axis=0, keepdims=True)
        start = pl.multiple_of(j * tk, tk)
        pv = jnp.dot(vt_ref[h, :, pl.ds(start, tk)], p.astype(BF16), preferred_element_type=F32)
        return m_new, l_new, corr * acc + pv

    for h, half in regions:
        kc = (i * tk + lax.broadcasted_iota(jnp.int32, (tk, hq), 0)) // CHUNK
        qc = (i * tq + half * hq + lax.broadcasted_iota(jnp.int32, (tk, hq), 1)) // CHUNK
        sbuf[h, :, half * hq:(half + 1) * hq] = jnp.where(kc <= qc, scores(h, half, i), -1e30)

    def body(jj, carry):
        j_old = jnp.where(jj == 0, i, jj - 1)
        out = []
        for r, (h, half) in enumerate(regions):
            s_new = scores(h, half, jj)
            out.append(consume(h, half, j_old, carry[r]))
            sbuf[h, :, half * hq:(half + 1) * hq] = s_new
        return tuple(out)

    init = tuple((jnp.full((1, hq), -1e30, F32), jnp.zeros((1, hq), F32), jnp.zeros((MLA_V, hq), F32))
                 for _ in regions)
    st = lax.fori_loop(0, i, body, init)
    j_last = jnp.maximum(i - 1, 0)
    for r, (h, half) in enumerate(regions):
        _, l_fin, acc = consume(h, half, j_last, st[r])
        ot = acc / l_fin
        o_ref[half * hq:(half + 1) * hq, h * MLA_V:(h + 1) * MLA_V] = ot.T.astype(BF16)


def _mla_layer(x, mods_i, w_dqkv, g_q, g_kv, w_uq, w_ukv, w_o, ln_g, ln_b, alpha, w_router, b_router,
               tm=512, tq=512, tk=512):
    bsz, s, d = x.shape
    nh = MLA_HEADS
    half = MLA_ROPE // 2
    k1 = w_dqkv[:, 640:640 + half]
    k2 = w_dqkv[:, 640 + half:704]
    z64 = jnp.zeros((d, 64), F32)
    wd = jnp.concatenate([w_dqkv[:, :640], k1, k2, z64, k2, k1, z64], axis=1).astype(BF16)
    wq = w_uq.reshape(MLA_Q_LORA, nh, MLA_NOPE + MLA_ROPE)
    wq_n = wq[:, :, :MLA_NOPE].reshape(MLA_Q_LORA, nh * 128)
    x1 = wq[:, :, MLA_NOPE:MLA_NOPE + half]
    x2 = wq[:, :, MLA_NOPE + half:]
    zq = jnp.zeros((MLA_Q_LORA, nh, 64), F32)
    wq_a = jnp.concatenate([x1, x2, zq], axis=2).reshape(MLA_Q_LORA, nh * 128)
    wq_b = jnp.concatenate([x2, x1, zq], axis=2).reshape(MLA_Q_LORA, nh * 128)
    wuqt = jnp.concatenate([wq_n, wq_a, wq_b], axis=1).T.astype(BF16)
    wkv = w_ukv.reshape(MLA_KV_LORA, nh, MLA_NOPE + MLA_V)
    wk = wkv[:, :, :MLA_NOPE].reshape(MLA_KV_LORA, nh * MLA_NOPE).astype(BF16)
    wvt = wkv[:, :, MLA_NOPE:].reshape(MLA_KV_LORA, nh * MLA_V).T.astype(BF16)
    pos = jnp.arange(s, dtype=F32)
    inv_freq = ROPE_THETA ** (-jnp.arange(0, MLA_ROPE, 2, dtype=F32) / MLA_ROPE)
    ang = pos[:, None] * inv_freq[None, :]
    cs, sn = jnp.cos(ang), jnp.sin(ang)
    zt = jnp.zeros((s, 64), F32)
    cos_t = jnp.concatenate([cs, cs, zt], axis=1)
    sin_t = jnp.concatenate([-sn, sn, zt], axis=1)
    qscale = (MLA_NOPE + MLA_ROPE) ** -0.5 * math.log2(math.e)

    qt, k, vt = pl.pallas_call(
        functools.partial(_mla_proj_kernel, qscale),
        grid=(bsz, s // tm),
        in_specs=[
            pl.BlockSpec((None, tm, d), lambda b, i: (b, i, 0)),
            pl.BlockSpec((None, 6, d), lambda b, i: (b, 0, 0)),
            pl.BlockSpec(wd.shape, lambda b, i: (0, 0)),
            pl.BlockSpec((1, MLA_Q_LORA), lambda b, i: (0, 0)),
            pl.BlockSpec((1, MLA_KV_LORA), lambda b, i: (0, 0)),
            pl.BlockSpec(wuqt.shape, lambda b, i: (0, 0)),
            pl.BlockSpec(wk.shape, lambda b, i: (0, 0)),
            pl.BlockSpec(wvt.shape, lambda b, i: (0, 0)),
            pl.BlockSpec((tm, 128), lambda b, i: (i, 0)),
            pl.BlockSpec((tm, 128), lambda b, i: (i, 0)),
            pl.BlockSpec((128, tm), lambda b, i: (0, i)),
            pl.BlockSpec((128, tm), lambda b, i: (0, i)),
        ],
        out_specs=[
            pl.BlockSpec((None, nh, MLA_QK, tm), lambda b, i: (b, 0, 0, i)),
            pl.BlockSpec((None, nh, tm, MLA_QK), lambda b, i: (b, 0, i, 0)),
            pl.BlockSpec((None, nh, MLA_V, tm), lambda b, i: (b, 0, 0, i)),
        ],
        out_shape=[
            jax.ShapeDtypeStruct((bsz, nh, MLA_QK, s), BF16),
            jax.ShapeDtypeStruct((bsz, nh, s, MLA_QK), BF16),
            jax.ShapeDtypeStruct((bsz, nh, MLA_V, s), BF16),
        ],
        compiler_params=_cparams(("arbitrary", "arbitrary")),
        name="mla_proj",
    )(x, mods_i, wd, g_q.reshape(1, -1), g_kv.reshape(1, -1), wuqt, wk, wvt, cos_t, sin_t, cos_t.T, sin_t.T)

    o = pl.pallas_call(
        functools.partial(_mla_attn_kernel, tq, tk),
        grid=(bsz, nh // MLA_HPS, s // tq),
        in_specs=[
            pl.BlockSpec((None, MLA_HPS, MLA_QK, tq), lambda b, h, i: (b, h, 0, i)),
            pl.BlockSpec((None, MLA_HPS, s, MLA_QK), lambda b, h, i: (b, h, 0, 0)),
            pl.BlockSpec((None, MLA_HPS, MLA_V, s), lambda b, h, i: (b, h, 0, 0)),
        ],
        out_specs=pl.BlockSpec((None, tq, MLA_HPS * MLA_V), lambda b, h, i: (b, i, h)),
        out_shape=jax.ShapeDtypeStruct((bsz, s, nh * MLA_V), BF16),
        scratch_shapes=[pltpu.VMEM((MLA_HPS, tk, tq), F32)],
        compiler_params=_cparams(("arbitrary", "arbitrary", "arbitrary")),
        name="mla_attn",
    )(qt, k, vt)

    return _post(o, w_o.astype(BF16), x, mods_i, ln_g, ln_b, alpha, 2, w_router, b_router)


def _pool_kernel(alpha, tm, x_ref, halo_ref, mod_ref, w_ref, cs_ref, g_ref, b_ref, wr_ref, br_ref,
                 o_ref, idx_ref, gate_ref, rank_ref, cnt_ref, buf, run):
    i = pl.program_id(1)
    d = x_ref.shape[-1]
    gd = d // len(POOL_WINDOWS)
    shift = mod_ref[0:1, :]
    scale = mod_ref[1:2, :]
    gate = mod_ref[2:3, :]
    x = x_ref[...]
    h = x * (1.0 + scale) + shift
    hh = halo_ref[...] * (1.0 + scale) + shift
    hh = jnp.where(i > 0, hh, 0.0)
    buf[0:POOL_HALO, :] = hh
    buf[POOL_HALO:, :] = h
    t = i * tm + lax.broadcasted_iota(jnp.int32, (tm, 1), 0)
    ys = []
    for g, win in enumerate(POOL_WINDOWS):
        cols = slice(g * gd, (g + 1) * gd)
        acc = buf[POOL_HALO:POOL_HALO + tm, cols]
        for j in range(1, win):
            acc = acc + buf[POOL_HALO - j:POOL_HALO - j + tm, cols]
        count = jnp.minimum(t + 1, win).astype(F32)
        dg = acc / count - h[:, cols]
        ys.append(_bdot(dg, w_ref[g]))
    y = jnp.concatenate(ys, axis=-1) * cs_ref[...]
    z = alpha * x + (1.0 + gate) * y
    x_new = _layer_norm(z, g_ref[...], b_ref[...])
    o_ref[...] = x_new
    _route_tile(tm, x_new, mod_ref, wr_ref, br_ref, idx_ref, gate_ref, rank_ref, cnt_ref, run)


def _pool_layer(x, mods_i, w_pool, ch_scale, ln_g, ln_b, alpha, w_router, b_router, tm=512):
    bsz, s, d = x.shape
    gd = d // len(POOL_WINDOWS)
    hb = tm // POOL_HALO
    r_in, r_out, r_shape, r_scratch = _route_io(bsz, s, d, tm)
    outs = pl.pallas_call(
        functools.partial(_pool_kernel, alpha, tm),
        grid=(bsz, s // tm),
        in_specs=[
            pl.BlockSpec((None, tm, d), lambda b, i: (b, i, 0)),
            pl.BlockSpec((None, POOL_HALO, d), lambda b, i: (b, jnp.maximum(i * hb - 1, 0), 0)),
            pl.BlockSpec((None, 6, d), lambda b, i: (b, 0, 0)),
            pl.BlockSpec((len(POOL_WINDOWS), gd, gd), lambda b, i: (0, 0, 0)),
            pl.BlockSpec((1, d), lambda b, i: (0, 0)),
            pl.BlockSpec((1, d), lambda b, i: (0, 0)),
            pl.BlockSpec((1, d), lambda b, i: (0, 0)),
        ] + r_in,
        out_specs=[pl.BlockSpec((None, tm, d), lambda b, i: (b, i, 0))] + r_out,
        out_shape=[jax.ShapeDtypeStruct((bsz, s, d), F32)] + r_shape,
        scratch_shapes=[pltpu.VMEM((tm + POOL_HALO, d), F32)] + r_scratch,
        compiler_params=_cparams(("arbitrary", "arbitrary")),
        name="pool_layer",
    )(x, x, mods_i, w_pool.astype(BF16), ch_scale.reshape(1, d), ln_g.reshape(1, d), ln_b.reshape(1, d),
      w_router.T, b_router.reshape(-1, 1))
    return outs[0], tuple(outs[1:])


def _gdn_proj_kernel(tm, x_ref, halo_ref, mod_ref, wqkv_ref, wz_ref, wba_ref, wc_ref, alog_ref, dtb_ref,
                     q_ref, k_ref, v_ref, z_ref, beta_ref, gc_ref, hbuf, pbuf):
    i = pl.program_id(1)
    shift = mod_ref[0:1, :]
    scale = mod_ref[1:2, :]
    h = x_ref[...] * (1.0 + scale) + shift
    hh = halo_ref[...] * (1.0 + scale) + shift
    hh = jnp.where(i > 0, hh, 0.0)
    hbuf[0:GDN_HALO, :] = hh
    hbuf[GDN_HALO:, :] = h
    h_ext = hbuf[...].astype(BF16)

    z_ref[...] = jnp.dot(h.astype(BF16), wz_ref[...], preferred_element_type=F32).astype(BF16)

    ba = _dot3(h, wba_ref[...])
    nv = GDN_V_HEADS
    beta_ref[...] = _sigmoid(ba[:, :nv])
    a = ba[:, nv:] + dtb_ref[...]
    softplus = jnp.maximum(a, 0.0) + jnp.log(1.0 + jnp.exp(-jnp.abs(a)))
    g = -jnp.exp(alog_ref[...]) * softplus
    r = lax.broadcasted_iota(jnp.int32, (tm, tm), 0)
    c = lax.broadcasted_iota(jnp.int32, (tm, tm), 1)
    tri = jnp.where((c <= r) & (c // CHUNK == r // CHUNK), 1.0, 0.0).astype(BF16)
    g_hi = g.astype(BF16)
    g_lo = (g - g_hi.astype(F32)).astype(BF16)
    g_lo2 = (g - g_hi.astype(F32) - g_lo.astype(F32)).astype(BF16)
    d = functools.partial(jnp.dot, preferred_element_type=F32)
    gc_ref[...] = d(tri, g_hi) + (d(tri, g_lo) + d(tri, g_lo2))

    cw = 512
    for cb in range(GDN_CONV_DIM // cw):
        cols = slice(cb * cw, (cb + 1) * cw)
        pbuf[...] = jnp.dot(h_ext, wqkv_ref[:, cols], preferred_element_type=F32)
        acc = pbuf[GDN_HALO:GDN_HALO + tm, :] * wc_ref[GDN_CONV - 1:GDN_CONV, cols]
        for j in range(GDN_CONV - 1):
            off = GDN_HALO - (GDN_CONV - 1) + j
            acc = acc + pbuf[off:off + tm, :] * wc_ref[j:j + 1, cols]
        y = _silu(acc)
        if cb * cw < 2 * GDN_Q_DIM:
            outs = []
            for hd in range(cw // GDN_DK):
                yh = y[:, hd * GDN_DK:(hd + 1) * GDN_DK]
                outs.append(yh * lax.rsqrt(jnp.sum(yh * yh, axis=-1, keepdims=True) + RMS_EPS))
            y = jnp.concatenate(outs, axis=-1)
            if cb * cw < GDN_Q_DIM:
                q_ref[:, cols] = (y * (GDN_DK ** -0.5)).astype(BF16)
            else:
                k_ref[:, cb * cw - GDN_Q_DIM:(cb + 1) * cw - GDN_Q_DIM] = y.astype(BF16)
        else:
            v_ref[:, cb * cw - 2 * GDN_Q_DIM:(cb + 1) * cw - 2 * GDN_Q_DIM] = y.astype(BF16)


def _col(x, hidx):
    lane = lax.broadcasted_iota(jnp.int32, x.shape, 1)
    return jnp.sum(jnp.where(lane == hidx, x, 0.0), axis=-1, keepdims=True)


GDN_HPS = 4


def _tile_rows(x, n):
    return jnp.concatenate([x] * n, axis=0)


def _gdn_chunk_kernel(tb, q_ref, k_ref, v_ref, z_ref, beta_ref, gc_ref, gr_ref, gn_ref, o_ref, s01, s23):
    hg = pl.program_id(1)
    n = pl.program_id(2)
    cs = CHUNK
    hp = GDN_HPS
    dv = GDN_DV
    wp = hp * cs
    wv = hp * dv

    @pl.when(n == 0)
    def _():
        s01[...] = jnp.zeros_like(s01)
        s23[...] = jnp.zeros_like(s23)

    bcols = [_col(beta_ref[...], hp * hg + i) for i in range(hp)]
    gcols = [_col(gc_ref[...], hp * hg + i) for i in range(hp)]
    beta_b = jnp.concatenate([jnp.broadcast_to(cc, (tb, dv)) for cc in bcols], axis=1)
    g_b = jnp.concatenate([jnp.broadcast_to(cc, (tb, dv)) for cc in gcols], axis=1)
    lane_p = lax.broadcasted_iota(jnp.int32, (tb, wp), 1)
    g_p = jnp.broadcast_to(gcols[hp - 1], (tb, wp))
    for i in reversed(range(hp - 1)):
        g_p = jnp.where(lane_p < (i + 1) * cs, gcols[i], g_p)

    r = lax.broadcasted_iota(jnp.int32, (cs, wp), 0)
    lc = lax.broadcasted_iota(jnp.int32, (cs, wp), 1) & (cs - 1)
    tril_incl = r >= lc
    tril_strict = r > lc
    eye_p = jnp.where(r == lc, 1.0, 0.0).astype(F32)
    rr = lax.broadcasted_iota(jnp.int32, (wp, wp), 0)
    ll = lax.broadcasted_iota(jnp.int32, (wp, wp), 1)
    bd_mask = (rr // cs) == (ll // cs)
    pair_mask = (rr // dv) == (ll // dv)
    rv = lax.broadcasted_iota(jnp.int32, (wp, wv), 0)
    lv = lax.broadcasted_iota(jnp.int32, (wp, wv), 1)
    bdv_mask = (rv // cs) == (lv // dv)

    def bd(y):
        return jnp.where(bd_mask, _tile_rows(y, hp), 0.0).astype(BF16)

    def bdv(y):
        return jnp.where(bdv_mask, _tile_rows(y, hp), 0.0).astype(BF16)

    def mm(a, b):
        return jnp.dot(a.astype(BF16), b, preferred_element_type=F32)

    def mm_nt(a, b):
        return lax.dot_general(a.astype(BF16), b, (((1,), (1,)), ((), ())), preferred_element_type=F32)

    states = [s01[...], s23[...]]
    gn = jnp.concatenate([gn_ref[...]] * hp, axis=1)
    nch = tb // cs
    pre = []
    for c in range(nch):
        rows = slice(c * cs, (c + 1) * cs)
        q2 = q_ref[rows, :].astype(F32)
        k2 = k_ref[rows, :].astype(F32)
        q4 = jnp.concatenate([q2[:, :dv], q2[:, :dv], q2[:, dv:], q2[:, dv:]], axis=1)
        k4 = jnp.concatenate([k2[:, :dv], k2[:, :dv], k2[:, dv:], k2[:, dv:]], axis=1)
        v4 = v_ref[rows, :].astype(F32)
        bb = beta_b[rows, :]
        gb = g_b[rows, :]
        gp = g_p[rows, :]
        grow = gr_ref[c:c + 1, :]
        decay = jnp.exp(jnp.where(tril_incl, gp - grow, -jnp.inf))
        eg = jnp.exp(gb)
        glast = gb[cs - 1:cs, :]
        kb = k4 * bb
        vb = v4 * bb
        kbd = bdv(k4)
        kk = mm_nt(kb, kbd)
        xm = -jnp.where(tril_strict, kk * decay, 0.0)
        a_intra = mm_nt(q4, kbd) * decay
        pre.append(dict(vbd=bdv(vb), kgd=bdv(kb * eg), a=a_intra, qg=q4 * eg, kd=k4 * jnp.exp(glast - gb),
                        gl=jnp.exp(glast), xm=xm))
    tinvs = [eye_p + pc["xm"] for pc in pre]
    pws = [pc["xm"] for pc in pre]
    bdps = [bd(pw) for pw in pws]
    for _ in range(5):
        pws = [mm(pw, bdp) for pw, bdp in zip(pws, bdps)]
        bdps = [bd(pw) for pw in pws]
        tinvs = [tinv + mm(tinv, bdp) for tinv, bdp in zip(tinvs, bdps)]
    us = [mm(tinv, pc["vbd"]) for tinv, pc in zip(tinvs, pre)]
    ws = [mm(tinv, pc["kgd"]) for tinv, pc in zip(tinvs, pre)]
    for c in range(nch):
        rows = slice(c * cs, (c + 1) * cs)
        u, w, a_intra, qg, kd, gl = us[c], ws[c], pre[c]["a"], pre[c]["qg"], pre[c]["kd"], pre[c]["gl"]
        vn, oq = [], []
        for p in range(2):
            sl = slice(p * 2 * dv, (p + 1) * 2 * dv)
            sb = states[p].astype(BF16)
            vn.append(u[:, sl] - mm(w[:, sl], sb))
            oq.append(mm(qg[:, sl], sb))
        v_new = jnp.concatenate(vn, axis=1)
        o = jnp.concatenate(oq, axis=1) + mm(a_intra, bdv(v_new))
        for p in range(2):
            sl = slice(p * 2 * dv, (p + 1) * 2 * dv)
            upd = lax.dot_general(kd[:, sl].astype(BF16), v_new[:, sl].astype(BF16), (((0,), (0,)), ((), ())),
                                  preferred_element_type=F32)
            states[p] = states[p] * gl[:, sl] + jnp.where(pair_mask, upd, 0.0)
        outs = []
        for i in range(hp):
            oi = o[:, i * dv:(i + 1) * dv]
            outs.append(oi * lax.rsqrt(jnp.mean(oi * oi, axis=-1, keepdims=True) + RMS_EPS))
        on = jnp.concatenate(outs, axis=1) * gn
        zz = z_ref[rows, :].astype(F32)
        o_ref[rows, :] = (on * _silu(zz)).astype(BF16)
    s01[...] = states[0]
    s23[...] = states[1]


def _gdn_layer(x, mods_i, w_in, w_conv, a_log, dt_bias, g_norm, w_o, ln_g, ln_b, alpha, w_router, b_router,
               tm=256, tb=512):
    bsz, s, d = x.shape
    nv = GDN_V_HEADS
    wqkv = w_in[:, :GDN_CONV_DIM].astype(BF16)
    wz = w_in[:, GDN_CONV_DIM:GDN_CONV_DIM + GDN_V_DIM].astype(BF16)
    wba = w_in[:, GDN_CONV_DIM + GDN_V_DIM:]
    hb = tm // GDN_HALO
    q, k, v, z, beta, gc = pl.pallas_call(
        functools.partial(_gdn_proj_kernel, tm),
        grid=(bsz, s // tm),
        in_specs=[
            pl.BlockSpec((None, tm, d), lambda b, i: (b, i, 0)),
            pl.BlockSpec((None, GDN_HALO, d), lambda b, i: (b, jnp.maximum(i * hb - 1, 0), 0)),
            pl.BlockSpec((None, 6, d), lambda b, i: (b, 0, 0)),
            pl.BlockSpec(wqkv.shape, lambda b, i: (0, 0)),
            pl.BlockSpec(wz.shape, lambda b, i: (0, 0)),
            pl.BlockSpec(wba.shape, lambda b, i: (0, 0)),
            pl.BlockSpec((GDN_CONV, GDN_CONV_DIM), lambda b, i: (0, 0)),
            pl.BlockSpec((1, nv), lambda b, i: (0, 0)),
            pl.BlockSpec((1, nv), lambda b, i: (0, 0)),
        ],
        out_specs=[
            pl.BlockSpec((None, tm, GDN_Q_DIM), lambda b, i: (b, i, 0)),
            pl.BlockSpec((None, tm, GDN_Q_DIM), lambda b, i: (b, i, 0)),
            pl.BlockSpec((None, tm, GDN_V_DIM), lambda b, i: (b, i, 0)),
            pl.BlockSpec((None, tm, GDN_V_DIM), lambda b, i: (b, i, 0)),
            pl.BlockSpec((None, tm, nv), lambda b, i: (b, i, 0)),
            pl.BlockSpec((None, tm, nv), lambda b, i: (b, i, 0)),
        ],
        out_shape=[
            jax.ShapeDtypeStruct((bsz, s, GDN_Q_DIM), BF16),
            jax.ShapeDtypeStruct((bsz, s, GDN_Q_DIM), BF16),
            jax.ShapeDtypeStruct((bsz, s, GDN_V_DIM), BF16),
            jax.ShapeDtypeStruct((bsz, s, GDN_V_DIM), BF16),
            jax.ShapeDtypeStruct((bsz, s, nv), F32),
            jax.ShapeDtypeStruct((bsz, s, nv), F32),
        ],
        scratch_shapes=[pltpu.VMEM((tm + GDN_HALO, d), F32), pltpu.VMEM((tm + GDN_HALO, 512), F32)],
        compiler_params=_cparams(("arbitrary", "arbitrary")),
        name="gdn_proj",
    )(x, x, mods_i, wqkv, wz, wba, w_conv, a_log.reshape(1, nv), dt_bias.reshape(1, nv))

    hp = GDN_HPS
    nc = s // CHUNK
    gr = gc.reshape(bsz, nc, CHUNK, nv // hp, hp).transpose(0, 3, 1, 4, 2).reshape(bsz, nv // hp, nc, hp * CHUNK)
    rep = GDN_V_HEADS // GDN_QK_HEADS
    cpb = tb // CHUNK
    qw = hp // rep * GDN_DK
    o = pl.pallas_call(
        functools.partial(_gdn_chunk_kernel, tb),
        grid=(bsz, nv // hp, s // tb),
        in_specs=[
            pl.BlockSpec((None, tb, qw), lambda b, h, n: (b, n, h)),
            pl.BlockSpec((None, tb, qw), lambda b, h, n: (b, n, h)),
            pl.BlockSpec((None, tb, hp * GDN_DV), lambda b, h, n: (b, n, h)),
            pl.BlockSpec((None, tb, hp * GDN_DV), lambda b, h, n: (b, n, h)),
            pl.BlockSpec((None, tb, nv), lambda b, h, n: (b, n, 0)),
            pl.BlockSpec((None, tb, nv), lambda b, h, n: (b, n, 0)),
            pl.BlockSpec((None, None, cpb, hp * CHUNK), lambda b, h, n: (b, h, n, 0)),
            pl.BlockSpec((1, GDN_DV), lambda b, h, n: (0, 0)),
        ],
        out_specs=pl.BlockSpec((None, tb, hp * GDN_DV), lambda b, h, n: (b, n, h)),
        out_shape=jax.ShapeDtypeStruct((bsz, s, GDN_V_DIM), BF16),
        scratch_shapes=[pltpu.VMEM((2 * GDN_DK, 2 * GDN_DV), F32), pltpu.VMEM((2 * GDN_DK, 2 * GDN_DV), F32)],
        compiler_params=_cparams(("arbitrary", "arbitrary", "arbitrary")),
        name="gdn_chunk",
    )(q, k, v, z, beta, gc, gr, g_norm.reshape(1, GDN_DV))

    return _post(o, w_o.astype(BF16), x, mods_i, ln_g, ln_b, alpha, 2, w_router, b_router)


def _route_tile(tm, x_new, mod_ref, wr_ref, br_ref, idx_ref, gate_ref, rank_ref, cnt_ref, run):
    step = pl.program_id(0) * pl.num_programs(1) + pl.program_id(1)

    @pl.when(step == 0)
    def _():
        run[...] = jnp.zeros_like(run)

    shift = mod_ref[3:4, :]
    scale = mod_ref[4:5, :]
    h = x_new * (1.0 + scale) + shift
    logits = _dot3_nt(wr_ref[...], h) + br_ref[...]
    eid = lax.broadcasted_iota(jnp.int32, (N_EXPERTS, tm), 0)
    vals, idxs, hots = [], [], []
    cur = logits
    for _ in range(TOP_K):
        m = jnp.max(cur, axis=0, keepdims=True)
        sel = jnp.min(jnp.where(cur == m, eid, N_EXPERTS), axis=0, keepdims=True)
        hot = eid == sel
        vals.append(m)
        idxs.append(sel)
        hots.append(hot)
        cur = jnp.where(hot, -jnp.inf, cur)
    es = [jnp.exp(v - vals[0]) for v in vals]
    den = es[0] + es[1] + es[2] + es[3]
    cnt = jnp.zeros((N_EXPERTS, tm), F32)
    for hot in hots:
        cnt = cnt + jnp.where(hot, 1.0, 0.0)
    r = lax.broadcasted_iota(jnp.int32, (tm, tm), 0)
    c = lax.broadcasted_iota(jnp.int32, (tm, tm), 1)
    before = jnp.where(r < c, 1.0, 0.0).astype(BF16)
    prefix = jnp.dot(cnt.astype(BF16), before, preferred_element_type=F32) + run[:, 0:1]
    for kk in range(TOP_K):
        idx_ref[kk:kk + 1, :] = idxs[kk]
        gate_ref[kk:kk + 1, :] = es[kk] / den
        rank_ref[kk:kk + 1, :] = jnp.sum(jnp.where(hots[kk], prefix, 0.0), axis=0, keepdims=True).astype(jnp.int32)
    run[...] = run[...] + jnp.sum(cnt, axis=1, keepdims=True)
    cnt_ref[...] = run[...]


def _dispatch_kernel(tm, dest_hbm, x_ref, mod_ref, zeros_hbm, rows_hbm, d_smem, hbuf, d_sem, s_sem):
    del zeros_hbm
    step = pl.program_id(0) * pl.num_programs(1) + pl.program_id(1)
    nsteps = pl.num_programs(0) * pl.num_programs(1)
    slot = step % 2
    nxt = 1 - slot

    def d_copy(blk, sl):
        return pltpu.make_async_copy(dest_hbm.at[blk], d_smem.at[sl], d_sem.at[sl])

    def wait_scatter(sl):
        for _ in range(TOP_K):
            pltpu.make_async_copy(hbuf.at[sl], rows_hbm.at[pl.ds(0, tm * ROW_TILE), :], s_sem.at[sl]).wait()

    @pl.when(step == 0)
    def _():
        d_copy(0, 0).start()

    @pl.when(step >= 2)
    def _():
        wait_scatter(slot)

    shift = mod_ref[3:4, :]
    scale = mod_ref[4:5, :]
    h = x_ref[...] * (1.0 + scale) + shift
    for sc in range(ROW_TILE):
        hbuf[slot, pl.ds(sc, tm, stride=ROW_TILE), :] = h[:, sc * 128:(sc + 1) * 128]

    d_copy(step, slot).wait()

    @pl.when(step + 1 < nsteps)
    def _():
        d_copy(step + 1, nxt).start()

    def body(r, carry):
        src = hbuf.at[slot, pl.ds(pl.multiple_of(r * ROW_TILE, ROW_TILE), ROW_TILE), :]
        for kk in range(TOP_K):
            dst = d_smem[slot, kk * tm + r]
            out = rows_hbm.at[pl.ds(pl.multiple_of(dst * ROW_TILE, ROW_TILE), ROW_TILE), :]
            pltpu.make_async_copy(src, out, s_sem.at[slot]).start(priority=kk % 2)
        return carry
    lax.fori_loop(0, tm, body, 0, unroll=4)

    @pl.when(step == nsteps - 1)
    def _():
        wait_scatter(slot)

        @pl.when(nsteps > 1)
        def _():
            wait_scatter(nxt)


def _expert_kernel(tm, be_ref, nx_ref, sl_ref, nu_ref, x_ref, wgu_hbm, bgu_ref, wd_hbm, bd_ref, y_ref,
                   wgu_f32, wd_f32, wgu_bf, wd_bf, w_sem):
    j = pl.program_id(0)
    n_used = nu_ref[0]
    changed = jnp.logical_or(j == 0, be_ref[j] != be_ref[jnp.maximum(j - 1, 0)])
    slot = sl_ref[j]

    def fetch(row, s):
        return (pltpu.make_async_copy(wgu_hbm.at[row], wgu_f32.at[s], w_sem.at[s, 0]),
                pltpu.make_async_copy(wd_hbm.at[row], wd_f32.at[s], w_sem.at[s, 1]))

    @pl.when(j == 0)
    def _():
        for cp in fetch(be_ref[0], 0):
            cp.start()

    @pl.when(jnp.logical_and(changed, j < n_used))
    def _():
        for cp in fetch(be_ref[j], slot):
            cp.wait()

        @pl.when(nx_ref[j] >= 0)
        def _():
            for cp in fetch(nx_ref[j], 1 - slot):
                cp.start()

        wgu_bf[...] = wgu_f32[slot].astype(BF16)
        wd_bf[...] = wd_f32[slot].astype(BF16)

    @pl.when(j >= n_used)
    def _():
        y_ref[...] = jnp.zeros_like(y_ref)

    @pl.when(j < n_used)
    def _():
        nsc = ROW_TILE
        xb = jnp.concatenate([x_ref[pl.ds(sc, tm, stride=nsc), :] for sc in range(nsc)], axis=-1).astype(BF16)
        dh = wd_bf.shape[0]
        cw = 256
        nck = dh // cw

        def gate_up(c):
            gcols = slice(c * cw, (c + 1) * cw)
            ucols = slice(dh + c * cw, dh + (c + 1) * cw)
            g = jnp.dot(xb, wgu_bf[:, gcols], preferred_element_type=F32) + bgu_ref[:, gcols]
            u = jnp.dot(xb, wgu_bf[:, ucols], preferred_element_type=F32) + bgu_ref[:, ucols]
            return g, u

        y = None
        nxt = gate_up(0)
        for c in range(nck):
            g, u = nxt
            if c + 1 < nck:
                nxt = gate_up(c + 1)
            gate = jnp.minimum(g, SWIGLU_LIMIT)
            up = jnp.clip(u, -SWIGLU_LIMIT, SWIGLU_LIMIT)
            act = (gate * _sigmoid(SWIGLU_ALPHA * gate) * (up + 1.0)).astype(BF16)
            part = jnp.dot(act, wd_bf[c * cw:(c + 1) * cw, :], preferred_element_type=F32)
            y = part if y is None else y + part
        y = y + bd_ref[...]
        for sc in range(nsc):
            y_ref[pl.ds(sc, tm, stride=nsc), :] = y[:, sc * 128:(sc + 1) * 128]


def _combine_kernel(alpha, tm, dest_hbm, y_hbm, x_ref, gate_ref, mod_ref, g_ref, b_ref, o_ref,
                    d_smem, ybuf, d_sem, y_sem):
    step = pl.program_id(0) * pl.num_programs(1) + pl.program_id(1)
    nsteps = pl.num_programs(0) * pl.num_programs(1)
    slot = step % 2
    nxt = 1 - slot

    def d_copy(blk, sl):
        return pltpu.make_async_copy(dest_hbm.at[blk], d_smem.at[sl], d_sem.at[sl])

    def issue_rows(sl):
        def body(r, carry):
            for kk in range(TOP_K):
                dst = d_smem[sl, kk * tm + r]
                src = y_hbm.at[pl.ds(pl.multiple_of(dst * ROW_TILE, ROW_TILE), ROW_TILE), :]
                dbuf = ybuf.at[sl, kk, pl.ds(pl.multiple_of(r * ROW_TILE, ROW_TILE), ROW_TILE), :]
                pltpu.make_async_copy(src, dbuf, y_sem.at[sl]).start(priority=kk % 2)
            return carry
        lax.fori_loop(0, tm, body, 0, unroll=4)

    @pl.when(step == 0)
    def _():
        d_copy(0, 0).start()
        d_copy(0, 0).wait()
        issue_rows(0)

        @pl.when(nsteps > 1)
        def _():
            d_copy(1, 1).start()

    @pl.when(step + 1 < nsteps)
    def _():
        d_copy(step + 1, nxt).wait()
        issue_rows(nxt)

    @pl.when(step + 2 < nsteps)
    def _():
        d_copy(step + 2, slot).start()

    for kk in range(TOP_K):
        pltpu.make_async_copy(y_hbm.at[pl.ds(0, tm * ROW_TILE), :], ybuf.at[slot, kk], y_sem.at[slot]).wait()
    g = gate_ref[...]
    parts = []
    for sc in range(ROW_TILE):
        acc = ybuf[slot, 0, pl.ds(sc, tm, stride=ROW_TILE), :] * g[:, 0:1]
        for kk in range(1, TOP_K):
            acc = acc + ybuf[slot, kk, pl.ds(sc, tm, stride=ROW_TILE), :] * g[:, kk:kk + 1]
        parts.append(acc)
    y = jnp.concatenate(parts, axis=-1)
    gate_f = mod_ref[5:6, :]
    z = alpha * x_ref[...] + (1.0 + gate_f) * y
    o_ref[...] = _layer_norm(z, g_ref[...], b_ref[...])


MOE_TE = 512


def _moe_rows_buffer(t):
    return jnp.zeros(((t * TOP_K + N_EXPERTS * MOE_TE) * ROW_TILE, 128), F32)


def _moe_layer(x, mods_i, routing, rows_buf, w_gate_up, b_gate_up, w_down, b_down, ln_g, ln_b, alpha,
               layer=0, tc=256):
    bsz, s, d = x.shape
    assert d == ROW_TILE * 128, "token rows are stored as one (8,128) f32 tile"
    t = bsz * s
    ne = N_EXPERTS
    te = MOE_TE
    idx, gates, rank, counts = routing

    cnt = counts[:, 0].astype(jnp.int32)
    padded = (cnt + te - 1) // te * te
    pend = jnp.cumsum(padded)
    pstart = pend - padded
    n_rows = t * TOP_K + ne * te
    n_blocks = n_rows // te
    eids = jnp.arange(ne, dtype=jnp.int32)
    dest = jnp.sum(jnp.where(idx[None] == eids[:, None, None], pstart[:, None, None], 0), axis=0) + rank
    n_used = (pend[-1] // te).astype(jnp.int32).reshape(1)
    blk_start = jnp.minimum(jnp.arange(n_blocks, dtype=jnp.int32) * te, pend[-1] - te)
    blk_expert = jnp.minimum(jnp.sum((blk_start[:, None] >= pend[None, :]).astype(jnp.int32), axis=1), ne - 1)
    present = padded > 0
    nxt = lax.cummin(jnp.where(present, eids, ne)[::-1])[::-1]
    next_present = jnp.concatenate([nxt[1:], jnp.full((1,), ne, jnp.int32)])
    run_id = jnp.cumsum(present.astype(jnp.int32)) - 1
    onehot = blk_expert[:, None] == eids[None, :]
    blk_next = jnp.sum(jnp.where(onehot, next_present[None, :], 0), axis=1)
    blk_slot = jnp.sum(jnp.where(onehot, run_id[None, :], 0), axis=1) % 2
    blk_next = jnp.where(blk_next < ne, blk_next + layer * ne, -1)
    blk_expert = blk_expert + layer * ne

    nct = s // tc
    dest_t = dest.reshape(TOP_K, t // tc, tc).transpose(1, 0, 2).reshape(t // tc, TOP_K * tc)
    x_rows = pl.pallas_call(
        functools.partial(_dispatch_kernel, tc),
        grid=(bsz, nct),
        in_specs=[
            pl.BlockSpec(memory_space=pl.ANY),
            pl.BlockSpec((None, tc, d), lambda b, i: (b, i, 0)),
            pl.BlockSpec((None, 6, d), lambda b, i: (b, 0, 0)),
            pl.BlockSpec(memory_space=pl.ANY),
        ],
        out_specs=pl.BlockSpec(memory_space=pl.ANY),
        scratch_shapes=[
            pltpu.SMEM((2, TOP_K * tc), jnp.int32),
            pltpu.VMEM((2, tc * ROW_TILE, 128), F32),
            pltpu.SemaphoreType.DMA((2,)),
            pltpu.SemaphoreType.DMA((2,)),
        ],
        out_shape=jax.ShapeDtypeStruct((n_rows * ROW_TILE, 128), F32),
        input_output_aliases={3: 0},
        compiler_params=_cparams(("arbitrary", "arbitrary")),
        name="moe_dispatch",
    )(dest_t, x, mods_i, rows_buf)

    dh2 = w_gate_up.shape[-1]
    w_gate_up = w_gate_up.reshape(-1, d, dh2)
    b_gate_up = b_gate_up.reshape(-1, 1, dh2)
    w_down = w_down.reshape(-1, dh2 // 2, d)
    b_down = b_down.reshape(-1, 1, d)
    y_rows = pl.pallas_call(
        functools.partial(_expert_kernel, te),
        grid_spec=pltpu.PrefetchScalarGridSpec(
            num_scalar_prefetch=4,
            grid=(n_blocks,),
            in_specs=[
                pl.BlockSpec((te * ROW_TILE, 128), lambda j, be, nx, sl, nu: (jnp.minimum(j, nu[0] - 1), 0)),
                pl.BlockSpec(memory_space=pl.ANY),
                pl.BlockSpec((None, 1, dh2), lambda j, be, nx, sl, nu: (be[j], 0, 0)),
                pl.BlockSpec(memory_space=pl.ANY),
                pl.BlockSpec((None, 1, d), lambda j, be, nx, sl, nu: (be[j], 0, 0)),
            ],
            out_specs=pl.BlockSpec((te * ROW_TILE, 128), lambda j, be, nx, sl, nu: (j, 0)),
            scratch_shapes=[
                pltpu.VMEM((2, d, dh2), F32),
                pltpu.VMEM((2, dh2 // 2, d), F32),
                pltpu.VMEM((d, dh2), BF16),
                pltpu.VMEM((dh2 // 2, d), BF16),
                pltpu.SemaphoreType.DMA((2, 2)),
            ],
        ),
        out_shape=jax.ShapeDtypeStruct((n_rows * ROW_TILE, 128), F32),
        compiler_params=_cparams(("arbitrary",)),
        name="moe_experts",
    )(blk_expert, blk_next, blk_slot, n_used, x_rows, w_gate_up, b_gate_up, w_down, b_down)

    gates_t = gates.T
    out = pl.pallas_call(
        functools.partial(_combine_kernel, alpha, tc),
        grid=(bsz, nct),
        in_specs=[
            pl.BlockSpec(memory_space=pl.ANY),
            pl.BlockSpec(memory_space=pl.ANY),
            pl.BlockSpec((None, tc, d), lambda b, i: (b, i, 0)),
            pl.BlockSpec((tc, TOP_K), lambda b, i: (b * nct + i, 0)),
            pl.BlockSpec((None, 6, d), lambda b, i: (b, 0, 0)),
            pl.BlockSpec((1, d), lambda b, i: (0, 0)),
            pl.BlockSpec((1, d), lambda b, i: (0, 0)),
        ],
        out_specs=pl.BlockSpec((None, tc, d), lambda b, i: (b, i, 0)),
        out_shape=jax.ShapeDtypeStruct((bsz, s, d), F32),
        scratch_shapes=[
            pltpu.SMEM((2, TOP_K * tc), jnp.int32),
            pltpu.VMEM((2, TOP_K, tc * ROW_TILE, 128), F32),
            pltpu.SemaphoreType.DMA((2,)),
            pltpu.SemaphoreType.DMA((2,)),
        ],
        compiler_params=_cparams(("arbitrary", "arbitrary")),
        name="moe_combine",
    )(dest_t, y_rows, x, gates_t, mods_i, ln_g.reshape(1, d), ln_b.reshape(1, d))
    return out, x_rows


def kernel(x, c, ada_w, ada_b, ln_g, ln_b, mla_w_dqkv, mla_g_q, mla_g_kv, mla_w_uq, mla_w_ukv, mla_w_o,
           pool_w, pool_scale, gdn_w_in, gdn_w_conv, gdn_a_log, gdn_dt_bias, gdn_g_norm, gdn_w_o,
           moe_w_router, moe_b_router, moe_w_gate_up, moe_b_gate_up, moe_w_down, moe_b_down):
    depth = ada_w.shape[0]
    alpha = (2.0 * depth) ** 0.25
    mods = _ada_mods(c, ada_w, ada_b)
    rows_buf = _moe_rows_buffer(x.shape[0] * x.shape[1])
    for i in range(depth):
        kind, j = i % 3, i // 3
        m_i = mods[i]
        wr, br = moe_w_router[i], moe_b_router[i]
        if kind == 0:
            x, routing = _mla_layer(x, m_i, mla_w_dqkv[j], mla_g_q[j], mla_g_kv[j], mla_w_uq[j], mla_w_ukv[j],
                                    mla_w_o[j], ln_g[i, 0], ln_b[i, 0], alpha, wr, br)
        elif kind == 1:
            x, routing = _pool_layer(x, m_i, pool_w[j], pool_scale[j], ln_g[i, 0], ln_b[i, 0], alpha, wr, br)
        else:
            x, routing = _gdn_layer(x, m_i, gdn_w_in[j], gdn_w_conv[j], gdn_a_log[j], gdn_dt_bias[j],
                                    gdn_g_norm[j], gdn_w_o[j], ln_g[i, 0], ln_b[i, 0], alpha, wr, br)
        x, rows_buf = _moe_layer(x, m_i, routing, rows_buf, moe_w_gate_up, moe_b_gate_up,
                                 moe_w_down, moe_b_down, ln_g[i, 1], ln_b[i, 1], alpha, layer=i)
    return x
```

```python
import functools
import math

import jax
import jax.numpy as jnp
from jax import lax
from jax.experimental import pallas as pl
from jax.experimental.pallas import tpu as pltpu

F32 = jnp.float32
BF16 = jnp.bfloat16

CHUNK = 64
LN_EPS = 1e-5
RMS_EPS = 1e-6

MLA_HEADS = 8
MLA_NOPE = 128
MLA_ROPE = 64
MLA_V = 128
MLA_Q_LORA = 384
MLA_KV_LORA = 256
ROPE_THETA = 10000.0
MLA_QK = 256

POOL_WINDOWS = (2, 4, 8, 16)
POOL_HALO = 16

GDN_QK_HEADS = 8
GDN_V_HEADS = 16
GDN_DK = 128
GDN_DV = 128
GDN_CONV = 4
GDN_Q_DIM = GDN_QK_HEADS * GDN_DK
GDN_V_DIM = GDN_V_HEADS * GDN_DV
GDN_CONV_DIM = 2 * GDN_Q_DIM + GDN_V_DIM
GDN_HALO = 8

N_EXPERTS = 32
TOP_K = 4
SWIGLU_LIMIT = 7.0
SWIGLU_ALPHA = 1.702

VMEM_LIMIT = 52 * 1024 * 1024
ROW_TILE = 8


def _cparams(sem):
    return pltpu.CompilerParams(dimension_semantics=sem, vmem_limit_bytes=VMEM_LIMIT)


def _bdot(a, b):
    return jnp.dot(a.astype(BF16), b.astype(BF16), preferred_element_type=F32)


def _bdot_nt(a, b):
    return lax.dot_general(a.astype(BF16), b.astype(BF16), (((1,), (1,)), ((), ())),
                           preferred_element_type=F32)


def _bdot_tn(a, b):
    return lax.dot_general(a.astype(BF16), b.astype(BF16), (((0,), (0,)), ((), ())),
                           preferred_element_type=F32)


def _split(a):
    hi = a.astype(BF16)
    lo = (a - hi.astype(F32)).astype(BF16)
    return hi, lo


def _dot3(a, b):
    ah, al = _split(a)
    bh, bl = _split(b)
    d = functools.partial(jnp.dot, preferred_element_type=F32)
    return d(ah, bh) + (d(ah, bl) + d(al, bh))


def _dot3_nt(a, b):
    ah, al = _split(a)
    bh, bl = _split(b)
    d = lambda x, y: lax.dot_general(x, y, (((1,), (1,)), ((), ())), preferred_element_type=F32)
    return d(ah, bh) + (d(ah, bl) + d(al, bh))


def _layer_norm(z, g, b):
    mu = jnp.mean(z, axis=-1, keepdims=True)
    zc = z - mu
    var = jnp.mean(zc * zc, axis=-1, keepdims=True)
    return zc * lax.rsqrt(var + LN_EPS) * g + b


def _sigmoid(x):
    return 1.0 / (1.0 + jnp.exp(-x))


def _silu(x):
    return x * _sigmoid(x)


def _ada_kernel(c_ref, w_ref, b_ref, o_ref):
    sc = _silu(c_ref[...])
    o_ref[...] = _dot3(sc, w_ref[...]) + b_ref[...]


def _ada_mods(c, ada_w, ada_b):
    depth, d, n = ada_w.shape
    bsz = c.shape[0]
    rows = 8
    tn = 2048
    c8 = jnp.zeros((rows, d), F32).at[:bsz].set(c)
    out = pl.pallas_call(
        _ada_kernel,
        grid=(depth, n // tn),
        in_specs=[
            pl.BlockSpec((rows, d), lambda l, j: (0, 0)),
            pl.BlockSpec((None, d, tn), lambda l, j: (l, 0, j)),
            pl.BlockSpec((None, 1, tn), lambda l, j: (l, 0, j)),
        ],
        out_specs=pl.BlockSpec((None, rows, tn), lambda l, j: (l, 0, j)),
        out_shape=jax.ShapeDtypeStruct((depth, rows, n), F32),
        compiler_params=_cparams(("arbitrary", "arbitrary")),
        name="ada_mods",
    )(c8, ada_w, ada_b.reshape(depth, 1, n))
    return out[:, :bsz].reshape(depth, bsz, 6, d)


def _route_io(bsz, s, d, tm):
    t = bsz * s
    nst = s // tm
    ne = N_EXPERTS
    in_specs = [pl.BlockSpec((ne, d), lambda b, i: (0, 0)), pl.BlockSpec((ne, 1), lambda b, i: (0, 0))]
    out_specs = [pl.BlockSpec((TOP_K, tm), lambda b, i: (0, b * nst + i))] * 3 + [
        pl.BlockSpec((ne, 128), lambda b, i: (0, 0))]
    out_shape = [jax.ShapeDtypeStruct((TOP_K, t), jnp.int32), jax.ShapeDtypeStruct((TOP_K, t), F32),
                 jax.ShapeDtypeStruct((TOP_K, t), jnp.int32), jax.ShapeDtypeStruct((ne, 128), F32)]
    return in_specs, out_specs, out_shape, [pltpu.VMEM((ne, 128), F32)]


def _post_kernel(alpha, gate_row, tm, y_ref, w_ref, x_ref, mod_ref, g_ref, b_ref, wr_ref, br_ref,
                 o_ref, idx_ref, gate_ref, rank_ref, cnt_ref, run):
    y = jnp.dot(y_ref[...], w_ref[...], preferred_element_type=F32)
    gate = mod_ref[gate_row:gate_row + 1, :]
    z = alpha * x_ref[...] + (1.0 + gate) * y
    x_new = _layer_norm(z, g_ref[...], b_ref[...])
    o_ref[...] = x_new
    _route_tile(tm, x_new, mod_ref, wr_ref, br_ref, idx_ref, gate_ref, rank_ref, cnt_ref, run)


def _post(y_pre, w_bf, x, mods_i, ln_g, ln_b, alpha, gate_row, w_router, b_router, tm=512):
    bsz, s, d = x.shape
    kd = y_pre.shape[-1]
    r_in, r_out, r_shape, r_scratch = _route_io(bsz, s, d, tm)
    outs = pl.pallas_call(
        functools.partial(_post_kernel, alpha, gate_row, tm),
        grid=(bsz, s // tm),
        in_specs=[
            pl.BlockSpec((None, tm, kd), lambda b, i: (b, i, 0)),
            pl.BlockSpec((kd, d), lambda b, i: (0, 0)),
            pl.BlockSpec((None, tm, d), lambda b, i: (b, i, 0)),
            pl.BlockSpec((None, 6, d), lambda b, i: (b, 0, 0)),
            pl.BlockSpec((1, d), lambda b, i: (0, 0)),
            pl.BlockSpec((1, d), lambda b, i: (0, 0)),
        ] + r_in,
        out_specs=[pl.BlockSpec((None, tm, d), lambda b, i: (b, i, 0))] + r_out,
        out_shape=[jax.ShapeDtypeStruct((bsz, s, d), F32)] + r_shape,
        scratch_shapes=r_scratch,
        compiler_params=_cparams(("arbitrary", "arbitrary")),
        name="post_proj_ln",
    )(y_pre, w_bf, x, mods_i, ln_g.reshape(1, d), ln_b.reshape(1, d), w_router.T, b_router.reshape(-1, 1))
    return outs[0], tuple(outs[1:])


def _mla_proj_kernel(qscale, x_ref, mod_ref, wd_ref, gq_ref, gkv_ref, wuqt_ref, wk_ref, wvt_ref, cos_ref, sin_ref,
                     cost_ref, sint_ref, qt_ref, k_ref, vt_ref):
    shift = mod_ref[0:1, :]
    scale = mod_ref[1:2, :]
    h = x_ref[...] * (1.0 + scale) + shift
    down = _bdot(h, wd_ref[...])
    cq = down[:, :MLA_Q_LORA]
    ckv = down[:, MLA_Q_LORA:MLA_Q_LORA + MLA_KV_LORA]
    kra = down[:, 640:768]
    krb = down[:, 768:896]
    cos = cos_ref[...]
    sin = sin_ref[...]
    k_rope = (kra * cos + krb * sin).astype(BF16)

    cqn = cq * lax.rsqrt(jnp.mean(cq * cq, axis=-1, keepdims=True) + RMS_EPS) * gq_ref[...]
    ckvn = ckv * lax.rsqrt(jnp.mean(ckv * ckv, axis=-1, keepdims=True) + RMS_EPS) * gkv_ref[...]
    cqn = cqn.astype(BF16)
    ckvn = ckvn.astype(BF16)
    cost = cost_ref[...]
    sint = sint_ref[...]
    nh = MLA_HEADS

    def nt(a, b):
        return lax.dot_general(a, b, (((1,), (1,)), ((), ())), preferred_element_type=F32)

    for hh in range(nh):
        qnt = nt(wuqt_ref[hh * 128:(hh + 1) * 128, :], cqn)
        qat = nt(wuqt_ref[(nh + hh) * 128:(nh + hh + 1) * 128, :], cqn)
        qbt = nt(wuqt_ref[(2 * nh + hh) * 128:(2 * nh + hh + 1) * 128, :], cqn)
        qt_ref[hh, 0:128, :] = (qnt * qscale).astype(BF16)
        qt_ref[hh, 128:256, :] = ((qat * cost + qbt * sint) * qscale).astype(BF16)
        kn = jnp.dot(ckvn, wk_ref[:, hh * 128:(hh + 1) * 128], preferred_element_type=F32)
        k_ref[hh, :, 0:128] = kn.astype(BF16)
        k_ref[hh, :, 128:256] = k_rope
        vt_ref[hh, :, :] = nt(wvt_ref[hh * 128:(hh + 1) * 128, :], ckvn).astype(BF16)


MLA_HPS = 4


def _mla_attn_kernel(tq, tk, qt_ref, k_ref, vt_ref, o_ref, sbuf):
    assert tk == tq
    i = pl.program_id(2)
    nhs = MLA_HPS
    hq = tq // 2
    regions = [(h, half) for h in range(nhs) for half in range(2)]
    qts = {(h, half): qt_ref[h, :, half * hq:(half + 1) * hq] for h, half in regions}

    def scores(h, half, j):
        start = pl.multiple_of(j * tk, tk)
        return jnp.dot(k_ref[h, pl.ds(start, tk), :], qts[(h, half)], preferred_element_type=F32)

    def consume(h, half, j, st):
        m_old, l_old, acc = st
        s = sbuf[h, :, half * hq:(half + 1) * hq]
        m_new = jnp.maximum(m_old, jnp.max(s, axis=0, keepdims=True))
        p = jnp.exp2(s - m_new)
        corr = jnp.exp2(m_old - m_new)
        l_new = corr * l_old + jnp.sum(p, axis=0, keepdims=True)
        start = pl.multiple_of(j * tk, tk)
        pv = jnp.dot(vt_ref[h, :, pl.ds(start, tk)], p.astype(BF16), preferred_element_type=F32)
        return m_new, l_new, corr * acc + pv

    for h, half in regions:
        kc = (i * tk + lax.broadcasted_iota(jnp.int32, (tk, hq), 0)) // CHUNK
        qc = (i * tq + half * hq + lax.broadcasted_iota(jnp.int32, (tk, hq), 1)) // CHUNK
        sbuf[h, :, half * hq:(half + 1) * hq] = jnp.where(kc <= qc, scores(h, half, i), -1e30)

    def body(jj, carry):
        j_old = jnp.where(jj == 0, i, jj - 1)
        out = []
        for r, (h, half) in enumerate(regions):
            s_new = scores(h, half, jj)
            out.append(consume(h, half, j_old, carry[r]))
            sbuf[h, :, half * hq:(half + 1) * hq] = s_new
        return tuple(out)

    init = tuple((jnp.full((1, hq), -1e30, F32), jnp.zeros((1, hq), F32), jnp.zeros((MLA_V, hq), F32))
                 for _ in regions)
    st = lax.fori_loop(0, i, body, init)
    j_last = jnp.maximum(i - 1, 0)
    for r, (h, half) in enumerate(regions):
        _, l_fin, acc = consume(h, half, j_last, st[r])
        ot = acc / l_fin
        o_ref[half * hq:(half + 1) * hq, h * MLA_V:(h + 1) * MLA_V] = ot.T.astype(BF16)


def _mla_layer(x, mods_i, w_dqkv, g_q, g_kv, w_uq, w_ukv, w_o, ln_g, ln_b, alpha, w_router, b_router,
               tm=512, tq=512, tk=512):
    bsz, s, d = x.shape
    nh = MLA_HEADS
    half = MLA_ROPE // 2
    k1 = w_dqkv[:, 640:640 + half]
    k2 = w_dqkv[:, 640 + half:704]
    z64 = jnp.zeros((d, 64), F32)
    wd = jnp.concatenate([w_dqkv[:, :640], k1, k2, z64, k2, k1, z64], axis=1).astype(BF16)
    wq = w_uq.reshape(MLA_Q_LORA, nh, MLA_NOPE + MLA_ROPE)
    wq_n = wq[:, :, :MLA_NOPE].reshape(MLA_Q_LORA, nh * 128)
    x1 = wq[:, :, MLA_NOPE:MLA_NOPE + half]
    x2 = wq[:, :, MLA_NOPE + half:]
    zq = jnp.zeros((MLA_Q_LORA, nh, 64), F32)
    wq_a = jnp.concatenate([x1, x2, zq], axis=2).reshape(MLA_Q_LORA, nh * 128)
    wq_b = jnp.concatenate([x2, x1, zq], axis=2).reshape(MLA_Q_LORA, nh * 128)
    wuqt = jnp.concatenate([wq_n, wq_a, wq_b], axis=1).T.astype(BF16)
    wkv = w_ukv.reshape(MLA_KV_LORA, nh, MLA_NOPE + MLA_V)
    wk = wkv[:, :, :MLA_NOPE].reshape(MLA_KV_LORA, nh * MLA_NOPE).astype(BF16)
    wvt = wkv[:, :, MLA_NOPE:].reshape(MLA_KV_LORA, nh * MLA_V).T.astype(BF16)
    pos = jnp.arange(s, dtype=F32)
    inv_freq = ROPE_THETA ** (-jnp.arange(0, MLA_ROPE, 2, dtype=F32) / MLA_ROPE)
    ang = pos[:, None] * inv_freq[None, :]
    cs, sn = jnp.cos(ang), jnp.sin(ang)
    zt = jnp.zeros((s, 64), F32)
    cos_t = jnp.concatenate([cs, cs, zt], axis=1)
    sin_t = jnp.concatenate([-sn, sn, zt], axis=1)
    qscale = (MLA_NOPE + MLA_ROPE) ** -0.5 * math.log2(math.e)

    qt, k, vt = pl.pallas_call(
        functools.partial(_mla_proj_kernel, qscale),
        grid=(bsz, s // tm),
        in_specs=[
            pl.BlockSpec((None, tm, d), lambda b, i: (b, i, 0)),
            pl.BlockSpec((None, 6, d), lambda b, i: (b, 0, 0)),
            pl.BlockSpec(wd.shape, lambda b, i: (0, 0)),
            pl.BlockSpec((1, MLA_Q_LORA), lambda b, i: (0, 0)),
            pl.BlockSpec((1, MLA_KV_LORA), lambda b, i: (0, 0)),
            pl.BlockSpec(wuqt.shape, lambda b, i: (0, 0)),
            pl.BlockSpec(wk.shape, lambda b, i: (0, 0)),
            pl.BlockSpec(wvt.shape, lambda b, i: (0, 0)),
            pl.BlockSpec((tm, 128), lambda b, i: (i, 0)),
            pl.BlockSpec((tm, 128), lambda b, i: (i, 0)),
            pl.BlockSpec((128, tm), lambda b, i: (0, i)),
            pl.BlockSpec((128, tm), lambda b, i: (0, i)),
        ],
        out_specs=[
            pl.BlockSpec((None, nh, MLA_QK, tm), lambda b, i: (b, 0, 0, i)),
            pl.BlockSpec((None, nh, tm, MLA_QK), lambda b, i: (b, 0, i, 0)),
            pl.BlockSpec((None, nh, MLA_V, tm), lambda b, i: (b, 0, 0, i)),
        ],
        out_shape=[
            jax.ShapeDtypeStruct((bsz, nh, MLA_QK, s), BF16),
            jax.ShapeDtypeStruct((bsz, nh, s, MLA_QK), BF16),
            jax.ShapeDtypeStruct((bsz, nh, MLA_V, s), BF16),
        ],
        compiler_params=_cparams(("arbitrary", "arbitrary")),
        name="mla_proj",
    )(x, mods_i, wd, g_q.reshape(1, -1), g_kv.reshape(1, -1), wuqt, wk, wvt, cos_t, sin_t, cos_t.T, sin_t.T)

    o = pl.pallas_call(
        functools.partial(_mla_attn_kernel, tq, tk),
        grid=(bsz, nh // MLA_HPS, s // tq),
        in_specs=[
            pl.BlockSpec((None, MLA_HPS, MLA_QK, tq), lambda b, h, i: (b, h, 0, i)),
            pl.BlockSpec((None, MLA_HPS, s, MLA_QK), lambda b, h, i: (b, h, 0, 0)),
            pl.BlockSpec((None, MLA_HPS, MLA_V, s), lambda b, h, i: (b, h, 0, 0)),
        ],
        out_specs=pl.BlockSpec((None, tq, MLA_HPS * MLA_V), lambda b, h, i: (b, i, h)),
        out_shape=jax.ShapeDtypeStruct((bsz, s, nh * MLA_V), BF16),
        scratch_shapes=[pltpu.VMEM((MLA_HPS, tk, tq), F32)],
        compiler_params=_cparams(("arbitrary", "arbitrary", "arbitrary")),
        name="mla_attn",
    )(qt, k, vt)

    return _post(o, w_o.astype(BF16), x, mods_i, ln_g, ln_b, alpha, 2, w_router, b_router)


def _pool_kernel(alpha, tm, x_ref, halo_ref, mod_ref, w_ref, cs_ref, g_ref, b_ref, wr_ref, br_ref,
                 o_ref, idx_ref, gate_ref, rank_ref, cnt_ref, buf, run):
    i = pl.program_id(1)
    d = x_ref.shape[-1]
    gd = d // len(POOL_WINDOWS)
    shift = mod_ref[0:1, :]
    scale = mod_ref[1:2, :]
    gate = mod_ref[2:3, :]
    x = x_ref[...]
    h = x * (1.0 + scale) + shift
    hh = halo_ref[...] * (1.0 + scale) + shift
    hh = jnp.where(i > 0, hh, 0.0)
    buf[0:POOL_HALO, :] = hh
    buf[POOL_HALO:, :] = h
    t = i * tm + lax.broadcasted_iota(jnp.int32, (tm, 1), 0)
    ys = []
    for g, win in enumerate(POOL_WINDOWS):
        cols = slice(g * gd, (g + 1) * gd)
        acc = buf[POOL_HALO:POOL_HALO + tm, cols]
        for j in range(1, win):
            acc = acc + buf[POOL_HALO - j:POOL_HALO - j + tm, cols]
        count = jnp.minimum(t + 1, win).astype(F32)
        dg = acc / count - h[:, cols]
        ys.append(_bdot(dg, w_ref[g]))
    y = jnp.concatenate(ys, axis=-1) * cs_ref[...]
    z = alpha * x + (1.0 + gate) * y
    x_new = _layer_norm(z, g_ref[...], b_ref[...])
    o_ref[...] = x_new
    _route_tile(tm, x_new, mod_ref, wr_ref, br_ref, idx_ref, gate_ref, rank_ref, cnt_ref, run)


def _pool_layer(x, mods_i, w_pool, ch_scale, ln_g, ln_b, alpha, w_router, b_router, tm=512):
    bsz, s, d = x.shape
    gd = d // len(POOL_WINDOWS)
    hb = tm // POOL_HALO
    r_in, r_out, r_shape, r_scratch = _route_io(bsz, s, d, tm)
    outs = pl.pallas_call(
        functools.partial(_pool_kernel, alpha, tm),
        grid=(bsz, s // tm),
        in_specs=[
            pl.BlockSpec((None, tm, d), lambda b, i: (b, i, 0)),
            pl.BlockSpec((None, POOL_HALO, d), lambda b, i: (b, jnp.maximum(i * hb - 1, 0), 0)),
            pl.BlockSpec((None, 6, d), lambda b, i: (b, 0, 0)),
            pl.BlockSpec((len(POOL_WINDOWS), gd, gd), lambda b, i: (0, 0, 0)),
            pl.BlockSpec((1, d), lambda b, i: (0, 0)),
            pl.BlockSpec((1, d), lambda b, i: (0, 0)),
            pl.BlockSpec((1, d), lambda b, i: (0, 0)),
        ] + r_in,
        out_specs=[pl.BlockSpec((None, tm, d), lambda b, i: (b, i, 0))] + r_out,
        out_shape=[jax.ShapeDtypeStruct((bsz, s, d), F32)] + r_shape,
        scratch_shapes=[pltpu.VMEM((tm + POOL_HALO, d), F32)] + r_scratch,
        compiler_params=_cparams(("arbitrary", "arbitrary")),
        name="pool_layer",
    )(x, x, mods_i, w_pool.astype(BF16), ch_scale.reshape(1, d), ln_g.reshape(1, d), ln_b.reshape(1, d),
      w_router.T, b_router.reshape(-1, 1))
    return outs[0], tuple(outs[1:])


def _gdn_proj_kernel(tm, x_ref, halo_ref, mod_ref, wqkv_ref, wz_ref, wba_ref, wc_ref, alog_ref, dtb_ref,
                     q_ref, k_ref, v_ref, z_ref, beta_ref, gc_ref, hbuf, pbuf):
    i = pl.program_id(1)
    shift = mod_ref[0:1, :]
    scale = mod_ref[1:2, :]
    h = x_ref[...] * (1.0 + scale) + shift
    hh = halo_ref[...] * (1.0 + scale) + shift
    hh = jnp.where(i > 0, hh, 0.0)
    hbuf[0:GDN_HALO, :] = hh
    hbuf[GDN_HALO:, :] = h
    h_ext = hbuf[...].astype(BF16)

    z_ref[...] = jnp.dot(h.astype(BF16), wz_ref[...], preferred_element_type=F32).astype(BF16)

    ba = _dot3(h, wba_ref[...])
    nv = GDN_V_HEADS
    beta_ref[...] = _sigmoid(ba[:, :nv])
    a = ba[:, nv:] + dtb_ref[...]
    softplus = jnp.maximum(a, 0.0) + jnp.log(1.0 + jnp.exp(-jnp.abs(a)))
    g = -jnp.exp(alog_ref[...]) * softplus
    r = lax.broadcasted_iota(jnp.int32, (tm, tm), 0)
    c = lax.broadcasted_iota(jnp.int32, (tm, tm), 1)
    tri = jnp.where((c <= r) & (c // CHUNK == r // CHUNK), 1.0, 0.0).astype(BF16)
    g_hi = g.astype(BF16)
    g_lo = (g - g_hi.astype(F32)).astype(BF16)
    g_lo2 = (g - g_hi.astype(F32) - g_lo.astype(F32)).astype(BF16)
    d = functools.partial(jnp.dot, preferred_element_type=F32)
    gc_ref[...] = d(tri, g_hi) + (d(tri, g_lo) + d(tri, g_lo2))

    cw = 512
    for cb in range(GDN_CONV_DIM // cw):
        cols = slice(cb * cw, (cb + 1) * cw)
        pbuf[...] = jnp.dot(h_ext, wqkv_ref[:, cols], preferred_element_type=F32)
        acc = pbuf[GDN_HALO:GDN_HALO + tm, :] * wc_ref[GDN_CONV - 1:GDN_CONV, cols]
        for j in range(GDN_CONV - 1):
            off = GDN_HALO - (GDN_CONV - 1) + j
            acc = acc + pbuf[off:off + tm, :] * wc_ref[j:j + 1, cols]
        y = _silu(acc)
        if cb * cw < 2 * GDN_Q_DIM:
            outs = []
            for hd in range(cw // GDN_DK):
                yh = y[:, hd * GDN_DK:(hd + 1) * GDN_DK]
                outs.append(yh * lax.rsqrt(jnp.sum(yh * yh, axis=-1, keepdims=True) + RMS_EPS))
            y = jnp.concatenate(outs, axis=-1)
            if cb * cw < GDN_Q_DIM:
                q_ref[:, cols] = (y * (GDN_DK ** -0.5)).astype(BF16)
            else:
                k_ref[:, cb * cw - GDN_Q_DIM:(cb + 1) * cw - GDN_Q_DIM] = y.astype(BF16)
        else:
            v_ref[:, cb * cw - 2 * GDN_Q_DIM:(cb + 1) * cw - 2 * GDN_Q_DIM] = y.astype(BF16)


def _col(x, hidx):
    lane = lax.broadcasted_iota(jnp.int32, x.shape, 1)
    return jnp.sum(jnp.where(lane == hidx, x, 0.0), axis=-1, keepdims=True)


GDN_HPS = 4


def _tile_rows(x, n):
    return jnp.concatenate([x] * n, axis=0)


def _gdn_chunk_kernel(tb, q_ref, k_ref, v_ref, z_ref, beta_ref, gc_ref, gr_ref, gn_ref, o_ref, s01, s23):
    hg = pl.program_id(1)
    n = pl.program_id(2)
    cs = CHUNK
    hp = GDN_HPS
    dv = GDN_DV
    wp = hp * cs
    wv = hp * dv

    @pl.when(n == 0)
    def _():
        s01[...] = jnp.zeros_like(s01)
        s23[...] = jnp.zeros_like(s23)

    bcols = [_col(beta_ref[...], hp * hg + i) for i in range(hp)]
    gcols = [_col(gc_ref[...], hp * hg + i) for i in range(hp)]
    beta_b = jnp.concatenate([jnp.broadcast_to(cc, (tb, dv)) for cc in bcols], axis=1)
    g_b = jnp.concatenate([jnp.broadcast_to(cc, (tb, dv)) for cc in gcols], axis=1)
    lane_p = lax.broadcasted_iota(jnp.int32, (tb, wp), 1)
    g_p = jnp.broadcast_to(gcols[hp - 1], (tb, wp))
    for i in reversed(range(hp - 1)):
        g_p = jnp.where(lane_p < (i + 1) * cs, gcols[i], g_p)

    r = lax.broadcasted_iota(jnp.int32, (cs, wp), 0)
    lc = lax.broadcasted_iota(jnp.int32, (cs, wp), 1) & (cs - 1)
    tril_incl = r >= lc
    tril_strict = r > lc
    eye_p = jnp.where(r == lc, 1.0, 0.0).astype(F32)
    rr = lax.broadcasted_iota(jnp.int32, (wp, wp), 0)
    ll = lax.broadcasted_iota(jnp.int32, (wp, wp), 1)
    bd_mask = (rr // cs) == (ll // cs)
    pair_mask = (rr // dv) == (ll // dv)
    rv = lax.broadcasted_iota(jnp.int32, (wp, wv), 0)
    lv = lax.broadcasted_iota(jnp.int32, (wp, wv), 1)
    bdv_mask = (rv // cs) == (lv // dv)

    def bd(y):
        return jnp.where(bd_mask, _tile_rows(y, hp), 0.0).astype(BF16)

    def bdv(y):
        return jnp.where(bdv_mask, _tile_rows(y, hp), 0.0).astype(BF16)

    def mm(a, b):
        return jnp.dot(a.astype(BF16), b, preferred_element_type=F32)

    def mm_nt(a, b):
        return lax.dot_general(a.astype(BF16), b, (((1,), (1,)), ((), ())), preferred_element_type=F32)

    states = [s01[...], s23[...]]
    gn = jnp.concatenate([gn_ref[...]] * hp, axis=1)
    nch = tb // cs
    pre = []
    for c in range(nch):
        rows = slice(c * cs, (c + 1) * cs)
        q2 = q_ref[rows, :].astype(F32)
        k2 = k_ref[rows, :].astype(F32)
        q4 = jnp.concatenate([q2[:, :dv], q2[:, :dv], q2[:, dv:], q2[:, dv:]], axis=1)
        k4 = jnp.concatenate([k2[:, :dv], k2[:, :dv], k2[:, dv:], k2[:, dv:]], axis=1)
        v4 = v_ref[rows, :].astype(F32)
        bb = beta_b[rows, :]
        gb = g_b[rows, :]
        gp = g_p[rows, :]
        grow = gr_ref[c:c + 1, :]
        decay = jnp.exp(jnp.where(tril_incl, gp - grow, -jnp.inf))
        eg = jnp.exp(gb)
        glast = gb[cs - 1:cs, :]
        kb = k4 * bb
        vb = v4 * bb
        kbd = bdv(k4)
        kk = mm_nt(kb, kbd)
        xm = -jnp.where(tril_strict, kk * decay, 0.0)
        a_intra = mm_nt(q4, kbd) * decay
        pre.append(dict(vbd=bdv(vb), kgd=bdv(kb * eg), a=a_intra, qg=q4 * eg, kd=k4 * jnp.exp(glast - gb),
                        gl=jnp.exp(glast), xm=xm))
    tinvs = [eye_p + pc["xm"] for pc in pre]
    pws = [pc["xm"] for pc in pre]
    bdps = [bd(pw) for pw in pws]
    for _ in range(5):
        pws = [mm(pw, bdp) for pw, bdp in zip(pws, bdps)]
        bdps = [bd(pw) for pw in pws]
        tinvs = [tinv + mm(tinv, bdp) for tinv, bdp in zip(tinvs, bdps)]
    us = [mm(tinv, pc["vbd"]) for tinv, pc in zip(tinvs, pre)]
    ws = [mm(tinv, pc["kgd"]) for tinv, pc in zip(tinvs, pre)]
    for c in range(nch):
        rows = slice(c * cs, (c + 1) * cs)
        u, w, a_intra, qg, kd, gl = us[c], ws[c], pre[c]["a"], pre[c]["qg"], pre[c]["kd"], pre[c]["gl"]
        vn, oq = [], []
        for p in range(2):
            sl = slice(p * 2 * dv, (p + 1) * 2 * dv)
            sb = states[p].astype(BF16)
            vn.append(u[:, sl] - mm(w[:, sl], sb))
            oq.append(mm(qg[:, sl], sb))
        v_new = jnp.concatenate(vn, axis=1)
        o = jnp.concatenate(oq, axis=1) + mm(a_intra, bdv(v_new))
        for p in range(2):
            sl = slice(p * 2 * dv, (p + 1) * 2 * dv)
            upd = lax.dot_general(kd[:, sl].astype(BF16), v_new[:, sl].astype(BF16), (((0,), (0,)), ((), ())),
                                  preferred_element_type=F32)
            states[p] = states[p] * gl[:, sl] + jnp.where(pair_mask, upd, 0.0)
        outs = []
        for i in range(hp):
            oi = o[:, i * dv:(i + 1) * dv]
            outs.append(oi * lax.rsqrt(jnp.mean(oi * oi, axis=-1, keepdims=True) + RMS_EPS))
        on = jnp.concatenate(outs, axis=1) * gn
        zz = z_ref[rows, :].astype(F32)
        o_ref[rows, :] = (on * _silu(zz)).astype(BF16)
    s01[...] = states[0]
    s23[...] = states[1]


def _gdn_layer(x, mods_i, w_in, w_conv, a_log, dt_bias, g_norm, w_o, ln_g, ln_b, alpha, w_router, b_router,
               tm=256, tb=512):
    bsz, s, d = x.shape
    nv = GDN_V_HEADS
    wqkv = w_in[:, :GDN_CONV_DIM].astype(BF16)
    wz = w_in[:, GDN_CONV_DIM:GDN_CONV_DIM + GDN_V_DIM].astype(BF16)
    wba = w_in[:, GDN_CONV_DIM + GDN_V_DIM:]
    hb = tm // GDN_HALO
    q, k, v, z, beta, gc = pl.pallas_call(
        functools.partial(_gdn_proj_kernel, tm),
        grid=(bsz, s // tm),
        in_specs=[
            pl.BlockSpec((None, tm, d), lambda b, i: (b, i, 0)),
            pl.BlockSpec((None, GDN_HALO, d), lambda b, i: (b, jnp.maximum(i * hb - 1, 0), 0)),
            pl.BlockSpec((None, 6, d), lambda b, i: (b, 0, 0)),
            pl.BlockSpec(wqkv.shape, lambda b, i: (0, 0)),
            pl.BlockSpec(wz.shape, lambda b, i: (0, 0)),
            pl.BlockSpec(wba.shape, lambda b, i: (0, 0)),
            pl.BlockSpec((GDN_CONV, GDN_CONV_DIM), lambda b, i: (0, 0)),
            pl.BlockSpec((1, nv), lambda b, i: (0, 0)),
            pl.BlockSpec((1, nv), lambda b, i: (0, 0)),
        ],
        out_specs=[
            pl.BlockSpec((None, tm, GDN_Q_DIM), lambda b, i: (b, i, 0)),
            pl.BlockSpec((None, tm, GDN_Q_DIM), lambda b, i: (b, i, 0)),
            pl.BlockSpec((None, tm, GDN_V_DIM), lambda b, i: (b, i, 0)),
            pl.BlockSpec((None, tm, GDN_V_DIM), lambda b, i: (b, i, 0)),
            pl.BlockSpec((None, tm, nv), lambda b, i: (b, i, 0)),
            pl.BlockSpec((None, tm, nv), lambda b, i: (b, i, 0)),
        ],
        out_shape=[
            jax.ShapeDtypeStruct((bsz, s, GDN_Q_DIM), BF16),
            jax.ShapeDtypeStruct((bsz, s, GDN_Q_DIM), BF16),
            jax.ShapeDtypeStruct((bsz, s, GDN_V_DIM), BF16),
            jax.ShapeDtypeStruct((bsz, s, GDN_V_DIM), BF16),
            jax.ShapeDtypeStruct((bsz, s, nv), F32),
            jax.ShapeDtypeStruct((bsz, s, nv), F32),
        ],
        scratch_shapes=[pltpu.VMEM((tm + GDN_HALO, d), F32), pltpu.VMEM((tm + GDN_HALO, 512), F32)],
        compiler_params=_cparams(("arbitrary", "arbitrary")),
        name="gdn_proj",
    )(x, x, mods_i, wqkv, wz, wba, w_conv, a_log.reshape(1, nv), dt_bias.reshape(1, nv))

    hp = GDN_HPS
    nc = s // CHUNK
    gr = gc.reshape(bsz, nc, CHUNK, nv // hp, hp).transpose(0, 3, 1, 4, 2).reshape(bsz, nv // hp, nc, hp * CHUNK)
    rep = GDN_V_HEADS // GDN_QK_HEADS
    cpb = tb // CHUNK
    qw = hp // rep * GDN_DK
    o = pl.pallas_call(
        functools.partial(_gdn_chunk_kernel, tb),
        grid=(bsz, nv // hp, s // tb),
        in_specs=[
            pl.BlockSpec((None, tb, qw), lambda b, h, n: (b, n, h)),
            pl.BlockSpec((None, tb, qw), lambda b, h, n: (b, n, h)),
            pl.BlockSpec((None, tb, hp * GDN_DV), lambda b, h, n: (b, n, h)),
            pl.BlockSpec((None, tb, hp * GDN_DV), lambda b, h, n: (b, n, h)),
            pl.BlockSpec((None, tb, nv), lambda b, h, n: (b, n, 0)),
            pl.BlockSpec((None, tb, nv), lambda b, h, n: (b, n, 0)),
            pl.BlockSpec((None, None, cpb, hp * CHUNK), lambda b, h, n: (b, h, n, 0)),
            pl.BlockSpec((1, GDN_DV), lambda b, h, n: (0, 0)),
        ],
        out_specs=pl.BlockSpec((None, tb, hp * GDN_DV), lambda b, h, n: (b, n, h)),
        out_shape=jax.ShapeDtypeStruct((bsz, s, GDN_V_DIM), BF16),
        scratch_shapes=[pltpu.VMEM((2 * GDN_DK, 2 * GDN_DV), F32), pltpu.VMEM((2 * GDN_DK, 2 * GDN_DV), F32)],
        compiler_params=_cparams(("arbitrary", "arbitrary", "arbitrary")),
        name="gdn_chunk",
    )(q, k, v, z, beta, gc, gr, g_norm.reshape(1, GDN_DV))

    return _post(o, w_o.astype(BF16), x, mods_i, ln_g, ln_b, alpha, 2, w_router, b_router)


def _route_tile(tm, x_new, mod_ref, wr_ref, br_ref, idx_ref, gate_ref, rank_ref, cnt_ref, run):
    step = pl.program_id(0) * pl.num_programs(1) + pl.program_id(1)

    @pl.when(step == 0)
    def _():
        run[...] = jnp.zeros_like(run)

    shift = mod_ref[3:4, :]
    scale = mod_ref[4:5, :]
    h = x_new * (1.0 + scale) + shift
    logits = _dot3_nt(wr_ref[...], h) + br_ref[...]
    eid = lax.broadcasted_iota(jnp.int32, (N_EXPERTS, tm), 0)
    vals, idxs, hots = [], [], []
    cur = logits
    for _ in range(TOP_K):
        m = jnp.max(cur, axis=0, keepdims=True)
        sel = jnp.min(jnp.where(cur == m, eid, N_EXPERTS), axis=0, keepdims=True)
        hot = eid == sel
        vals.append(m)
        idxs.append(sel)
        hots.append(hot)
        cur = jnp.where(hot, -jnp.inf, cur)
    es = [jnp.exp(v - vals[0]) for v in vals]
    den = es[0] + es[1] + es[2] + es[3]
    cnt = jnp.zeros((N_EXPERTS, tm), F32)
    for hot in hots:
        cnt = cnt + jnp.where(hot, 1.0, 0.0)
    r = lax.broadcasted_iota(jnp.int32, (tm, tm), 0)
    c = lax.broadcasted_iota(jnp.int32, (tm, tm), 1)
    before = jnp.where(r < c, 1.0, 0.0).astype(BF16)
    prefix = jnp.dot(cnt.astype(BF16), before, preferred_element_type=F32) + run[:, 0:1]
    for kk in range(TOP_K):
        idx_ref[kk:kk + 1, :] = idxs[kk]
        gate_ref[kk:kk + 1, :] = es[kk] / den
        rank_ref[kk:kk + 1, :] = jnp.sum(jnp.where(hots[kk], prefix, 0.0), axis=0, keepdims=True).astype(jnp.int32)
    run[...] = run[...] + jnp.sum(cnt, axis=1, keepdims=True)
    cnt_ref[...] = run[...]


def _dispatch_kernel(tm, dest_hbm, x_ref, mod_ref, zeros_hbm, rows_hbm, d_smem, hbuf, d_sem, s_sem):
    del zeros_hbm
    step = pl.program_id(0) * pl.num_programs(1) + pl.program_id(1)
    nsteps = pl.num_programs(0) * pl.num_programs(1)
    slot = step % 2
    nxt = 1 - slot

    def d_copy(blk, sl):
        return pltpu.make_async_copy(dest_hbm.at[blk], d_smem.at[sl], d_sem.at[sl])

    def wait_scatter(sl):
        for _ in range(TOP_K):
            pltpu.make_async_copy(hbuf.at[sl], rows_hbm.at[pl.ds(0, tm * ROW_TILE), :], s_sem.at[sl]).wait()

    @pl.when(step == 0)
    def _():
        d_copy(0, 0).start()

    @pl.when(step >= 2)
    def _():
        wait_scatter(slot)

    shift = mod_ref[3:4, :]
    scale = mod_ref[4:5, :]
    h = x_ref[...] * (1.0 + scale) + shift
    for sc in range(ROW_TILE):
        hbuf[slot, pl.ds(sc, tm, stride=ROW_TILE), :] = h[:, sc * 128:(sc + 1) * 128]

    d_copy(step, slot).wait()

    @pl.when(step + 1 < nsteps)
    def _():
        d_copy(step + 1, nxt).start()

    def body(r, carry):
        src = hbuf.at[slot, pl.ds(pl.multiple_of(r * ROW_TILE, ROW_TILE), ROW_TILE), :]
        for kk in range(TOP_K):
            dst = d_smem[slot, kk * tm + r]
            out = rows_hbm.at[pl.ds(pl.multiple_of(dst * ROW_TILE, ROW_TILE), ROW_TILE), :]
            pltpu.make_async_copy(src, out, s_sem.at[slot]).start(priority=kk % 2)
        return carry
    lax.fori_loop(0, tm, body, 0, unroll=4)

    @pl.when(step == nsteps - 1)
    def _():
        wait_scatter(slot)

        @pl.when(nsteps > 1)
        def _():
            wait_scatter(nxt)


def _expert_kernel(tm, be_ref, nx_ref, sl_ref, nu_ref, x_ref, wgu_hbm, bgu_ref, wd_hbm, bd_ref, y_ref,
                   wgu_f32, wd_f32, wgu_bf, wd_bf, w_sem):
    j = pl.program_id(0)
    n_used = nu_ref[0]
    changed = jnp.logical_or(j == 0, be_ref[j] != be_ref[jnp.maximum(j - 1, 0)])
    slot = sl_ref[j]

    def fetch(row, s):
        return (pltpu.make_async_copy(wgu_hbm.at[row], wgu_f32.at[s], w_sem.at[s, 0]),
                pltpu.make_async_copy(wd_hbm.at[row], wd_f32.at[s], w_sem.at[s, 1]))

    @pl.when(j == 0)
    def _():
        for cp in fetch(be_ref[0], 0):
            cp.start()

    @pl.when(jnp.logical_and(changed, j < n_used))
    def _():
        for cp in fetch(be_ref[j], slot):
            cp.wait()

        @pl.when(nx_ref[j] >= 0)
        def _():
            for cp in fetch(nx_ref[j], 1 - slot):
                cp.start()

        wgu_bf[...] = wgu_f32[slot].astype(BF16)
        wd_bf[...] = wd_f32[slot].astype(BF16)

    @pl.when(j >= n_used)
    def _():
        y_ref[...] = jnp.zeros_like(y_ref)

    @pl.when(j < n_used)
    def _():
        nsc = ROW_TILE
        xb = jnp.concatenate([x_ref[pl.ds(sc, tm, stride=nsc), :] for sc in range(nsc)], axis=-1).astype(BF16)
        dh = wd_bf.shape[0]
        cw = 256
        nck = dh // cw

        def gate_up(c):
            gcols = slice(c * cw, (c + 1) * cw)
            ucols = slice(dh + c * cw, dh + (c + 1) * cw)
            g = jnp.dot(xb, wgu_bf[:, gcols], preferred_element_type=F32) + bgu_ref[:, gcols]
            u = jnp.dot(xb, wgu_bf[:, ucols], preferred_element_type=F32) + bgu_ref[:, ucols]
            return g, u

        y = None
        nxt = gate_up(0)
        for c in range(nck):
            g, u = nxt
            if c + 1 < nck:
                nxt = gate_up(c + 1)
            gate = jnp.minimum(g, SWIGLU_LIMIT)
            up = jnp.clip(u, -SWIGLU_LIMIT, SWIGLU_LIMIT)
            act = (gate * _sigmoid(SWIGLU_ALPHA * gate) * (up + 1.0)).astype(BF16)
            part = jnp.dot(act, wd_bf[c * cw:(c + 1) * cw, :], preferred_element_type=F32)
            y = part if y is None else y + part
        y = y + bd_ref[...]
        for sc in range(nsc):
            y_ref[pl.ds(sc, tm, stride=nsc), :] = y[:, sc * 128:(sc + 1) * 128]


def _combine_kernel(alpha, tm, dest_hbm, y_hbm, x_ref, gate_ref, mod_ref, g_ref, b_ref, o_ref,
                    d_smem, ybuf, d_sem, y_sem):
    step = pl.program_id(0) * pl.num_programs(1) + pl.program_id(1)
    nsteps = pl.num_programs(0) * pl.num_programs(1)
    slot = step % 2
    nxt = 1 - slot

    def d_copy(blk, sl):
        return pltpu.make_async_copy(dest_hbm.at[blk], d_smem.at[sl], d_sem.at[sl])

    def issue_rows(sl):
        def body(r, carry):
            for kk in range(TOP_K):
                dst = d_smem[sl, kk * tm + r]
                src = y_hbm.at[pl.ds(pl.multiple_of(dst * ROW_TILE, ROW_TILE), ROW_TILE), :]
                dbuf = ybuf.at[sl, kk, pl.ds(pl.multiple_of(r * ROW_TILE, ROW_TILE), ROW_TILE), :]
                pltpu.make_async_copy(src, dbuf, y_sem.at[sl]).start(priority=kk % 2)
            return carry
        lax.fori_loop(0, tm, body, 0, unroll=4)

    @pl.when(step == 0)
    def _():
        d_copy(0, 0).start()
        d_copy(0, 0).wait()
        issue_rows(0)

        @pl.when(nsteps > 1)
        def _():
            d_copy(1, 1).start()

    @pl.when(step + 1 < nsteps)
    def _():
        d_copy(step + 1, nxt).wait()
        issue_rows(nxt)

    @pl.when(step + 2 < nsteps)
    def _():
        d_copy(step + 2, slot).start()

    for kk in range(TOP_K):
        pltpu.make_async_copy(y_hbm.at[pl.ds(0, tm * ROW_TILE), :], ybuf.at[slot, kk], y_sem.at[slot]).wait()
    g = gate_ref[...]
    parts = []
    for sc in range(ROW_TILE):
        acc = ybuf[slot, 0, pl.ds(sc, tm, stride=ROW_TILE), :] * g[:, 0:1]
        for kk in range(1, TOP_K):
            acc = acc + ybuf[slot, kk, pl.ds(sc, tm, stride=ROW_TILE), :] * g[:, kk:kk + 1]
        parts.append(acc)
    y = jnp.concatenate(parts, axis=-1)
    gate_f = mod_ref[5:6, :]
    z = alpha * x_ref[...] + (1.0 + gate_f) * y
    o_ref[...] = _layer_norm(z, g_ref[...], b_ref[...])


MOE_TE = 512


def _moe_rows_buffer(t):
    return jnp.zeros(((t * TOP_K + N_EXPERTS * MOE_TE) * ROW_TILE, 128), F32)


def _moe_layer(x, mods_i, routing, rows_buf, w_gate_up, b_gate_up, w_down, b_down, ln_g, ln_b, alpha,
               layer=0, tc=256):
    bsz, s, d = x.shape
    assert d == ROW_TILE * 128, "token rows are stored as one (8,128) f32 tile"
    t = bsz * s
    ne = N_EXPERTS
    te = MOE_TE
    idx, gates, rank, counts = routing

    cnt = counts[:, 0].astype(jnp.int32)
    padded = (cnt + te - 1) // te * te
    pend = jnp.cumsum(padded)
    pstart = pend - padded
    n_rows = t * TOP_K + ne * te
    n_blocks = n_rows // te
    eids = jnp.arange(ne, dtype=jnp.int32)
    dest = jnp.sum(jnp.where(idx[None] == eids[:, None, None], pstart[:, None, None], 0), axis=0) + rank
    n_used = (pend[-1] // te).astype(jnp.int32).reshape(1)
    blk_start = jnp.minimum(jnp.arange(n_blocks, dtype=jnp.int32) * te, pend[-1] - te)
    blk_expert = jnp.minimum(jnp.sum((blk_start[:, None] >= pend[None, :]).astype(jnp.int32), axis=1), ne - 1)
    present = padded > 0
    nxt = lax.cummin(jnp.where(present, eids, ne)[::-1])[::-1]
    next_present = jnp.concatenate([nxt[1:], jnp.full((1,), ne, jnp.int32)])
    run_id = jnp.cumsum(present.astype(jnp.int32)) - 1
    onehot = blk_expert[:, None] == eids[None, :]
    blk_next = jnp.sum(jnp.where(onehot, next_present[None, :], 0), axis=1)
    blk_slot = jnp.sum(jnp.where(onehot, run_id[None, :], 0), axis=1) % 2
    blk_next = jnp.where(blk_next < ne, blk_next + layer * ne, -1)
    blk_expert = blk_expert + layer * ne

    nct = s // tc
    dest_t = dest.reshape(TOP_K, t // tc, tc).transpose(1, 0, 2).reshape(t // tc, TOP_K * tc)
    x_rows = pl.pallas_call(
        functools.partial(_dispatch_kernel, tc),
        grid=(bsz, nct),
        in_specs=[
            pl.BlockSpec(memory_space=pl.ANY),
            pl.BlockSpec((None, tc, d), lambda b, i: (b, i, 0)),
            pl.BlockSpec((None, 6, d), lambda b, i: (b, 0, 0)),
            pl.BlockSpec(memory_space=pl.ANY),
        ],
        out_specs=pl.BlockSpec(memory_space=pl.ANY),
        scratch_shapes=[
            pltpu.SMEM((2, TOP_K * tc), jnp.int32),
            pltpu.VMEM((2, tc * ROW_TILE, 128), F32),
            pltpu.SemaphoreType.DMA((2,)),
            pltpu.SemaphoreType.DMA((2,)),
        ],
        out_shape=jax.ShapeDtypeStruct((n_rows * ROW_TILE, 128), F32),
        input_output_aliases={3: 0},
        compiler_params=_cparams(("arbitrary", "arbitrary")),
        name="moe_dispatch",
    )(dest_t, x, mods_i, rows_buf)

    dh2 = w_gate_up.shape[-1]
    w_gate_up = w_gate_up.reshape(-1, d, dh2)
    b_gate_up = b_gate_up.reshape(-1, 1, dh2)
    w_down = w_down.reshape(-1, dh2 // 2, d)
    b_down = b_down.reshape(-1, 1, d)
    y_rows = pl.pallas_call(
        functools.partial(_expert_kernel, te),
        grid_spec=pltpu.PrefetchScalarGridSpec(
            num_scalar_prefetch=4,
            grid=(n_blocks,),
            in_specs=[
                pl.BlockSpec((te * ROW_TILE, 128), lambda j, be, nx, sl, nu: (jnp.minimum(j, nu[0] - 1), 0)),
                pl.BlockSpec(memory_space=pl.ANY),
                pl.BlockSpec((None, 1, dh2), lambda j, be, nx, sl, nu: (be[j], 0, 0)),
                pl.BlockSpec(memory_space=pl.ANY),
                pl.BlockSpec((None, 1, d), lambda j, be, nx, sl, nu: (be[j], 0, 0)),
            ],
            out_specs=pl.BlockSpec((te * ROW_TILE, 128), lambda j, be, nx, sl, nu: (j, 0)),
            scratch_shapes=[
                pltpu.VMEM((2, d, dh2), F32),
                pltpu.VMEM((2, dh2 // 2, d), F32),
                pltpu.VMEM((d, dh2), BF16),
                pltpu.VMEM((dh2 // 2, d), BF16),
                pltpu.SemaphoreType.DMA((2, 2)),
            ],
        ),
        out_shape=jax.ShapeDtypeStruct((n_rows * ROW_TILE, 128), F32),
        compiler_params=_cparams(("arbitrary",)),
        name="moe_experts",
    )(blk_expert, blk_next, blk_slot, n_used, x_rows, w_gate_up, b_gate_up, w_down, b_down)

    gates_t = gates.T
    out = pl.pallas_call(
        functools.partial(_combine_kernel, alpha, tc),
        grid=(bsz, nct),
        in_specs=[
            pl.BlockSpec(memory_space=pl.ANY),
            pl.BlockSpec(memory_space=pl.ANY),
            pl.BlockSpec((None, tc, d), lambda b, i: (b, i, 0)),
            pl.BlockSpec((tc, TOP_K), lambda b, i: (b * nct + i, 0)),
            pl.BlockSpec((None, 6, d), lambda b, i: (b, 0, 0)),
            pl.BlockSpec((1, d), lambda b, i: (0, 0)),
            pl.BlockSpec((1, d), lambda b, i: (0, 0)),
        ],
        out_specs=pl.BlockSpec((None, tc, d), lambda b, i: (b, i, 0)),
        out_shape=jax.ShapeDtypeStruct((bsz, s, d), F32),
        scratch_shapes=[
            pltpu.SMEM((2, TOP_K * tc), jnp.int32),
            pltpu.VMEM((2, TOP_K, tc * ROW_TILE, 128), F32),
            pltpu.SemaphoreType.DMA((2,)),
            pltpu.SemaphoreType.DMA((2,)),
        ],
        compiler_params=_cparams(("arbitrary", "arbitrary")),
        name="moe_combine",
    )(dest_t, y_rows, x, gates_t, mods_i, ln_g.reshape(1, d), ln_b.reshape(1, d))
    return out, x_rows


def kernel(x, c, ada_w, ada_b, ln_g, ln_b, mla_w_dqkv, mla_g_q, mla_g_kv, mla_w_uq, mla_w_ukv, mla_w_o,
           pool_w, pool_scale, gdn_w_in, gdn_w_conv, gdn_a_log, gdn_dt_bias, gdn_g_norm, gdn_w_o,
           moe_w_router, moe_b_router, moe_w_gate_up, moe_b_gate_up, moe_w_down, moe_b_down):
    depth = ada_w.shape[0]
    alpha = (2.0 * depth) ** 0.25
    mods = _ada_mods(c, ada_w, ada_b)
    rows_buf = _moe_rows_buffer(x.shape[0] * x.shape[1])
    for i in range(depth):
        kind, j = i % 3, i // 3
        m_i = mods[i]
        wr, br = moe_w_router[i], moe_b_router[i]
        if kind == 0:
            x, routing = _mla_layer(x, m_i, mla_w_dqkv[j], mla_g_q[j], mla_g_kv[j], mla_w_uq[j], mla_w_ukv[j],
                                    mla_w_o[j], ln_g[i, 0], ln_b[i, 0], alpha, wr, br)
        elif kind == 1:
            x, routing = _pool_layer(x, m_i, pool_w[j], pool_scale[j], ln_g[i, 0], ln_b[i, 0], alpha, wr, br)
        else:
            x, routing = _gdn_layer(x, m_i, gdn_w_in[j], gdn_w_conv[j], gdn_a_log[j], gdn_dt_bias[j],
                                    gdn_g_norm[j], gdn_w_o[j], ln_g[i, 0], ln_b[i, 0], alpha, wr, br)
        x, rows_buf = _moe_layer(x, m_i, routing, rows_buf, moe_w_gate_up, moe_b_gate_up,
                                 moe_w_down, moe_b_down, ln_g[i, 1], ln_b[i, 1], alpha, layer=i)
    return x
```
